```python
import math
import jax, jax.numpy as jnp
from jax import lax
import numpy as np

D_MODEL = 1024
BATCH = 4
SEQ = 4096
DEPTH = 1

MIX_WIDTH = D_MODEL
DIFF_HEADS = 4
DIFF_HEAD_DIM = 64
DIFF_V_DIM = 2 * DIFF_HEAD_DIM
RET_HEADS = 4
RET_QK_DIM = 64
RET_V_DIM = 2 * RET_QK_DIM
DIFF_Q_W = DIFF_HEADS * 2 * DIFF_HEAD_DIM
DIFF_V_W = DIFF_HEADS * DIFF_V_DIM
RET_QK_W = RET_HEADS * RET_QK_DIM
RET_V_W = RET_HEADS * RET_V_DIM
IN_SPLITS = (DIFF_Q_W, DIFF_Q_W, DIFF_V_W, RET_QK_W, RET_QK_W, RET_V_W, RET_V_W, D_MODEL, D_MODEL)
IN_COLS = sum(IN_SPLITS)
D_FF = ((8 * D_MODEL // 3 + 255) // 256) * 256
ROPE_THETA = 10000.0
Q_BLOCK = 128
RET_CHUNK = 128
NORM_EPS = 1e-5

kernel_name = "hybrid_diffattn_retention_gated_block"


def rmsnorm(x, g=None, eps=NORM_EPS):
    x32 = x.astype(jnp.float32)
    y = x32 * lax.rsqrt(jnp.mean(x32 * x32, axis=-1, keepdims=True) + eps)
    if g is not None:
        y = y * g.astype(jnp.float32)
    return y.astype(x.dtype)


def rope_tables(seq_len, dim):
    inv_freq = ROPE_THETA ** (-jnp.arange(0, dim, 2, dtype=jnp.float32) / dim)
    ang = jnp.arange(seq_len, dtype=jnp.float32)[:, None] * inv_freq[None, :]
    return jnp.cos(ang), jnp.sin(ang)


def apply_rope(x, cos, sin):
    half = x.shape[-1] // 2
    shp = (x.shape[1],) + (1,) * (x.ndim - 3) + (half,)
    c = cos.reshape(shp).astype(x.dtype)
    s = sin.reshape(shp).astype(x.dtype)
    x1, x2 = x[..., :half], x[..., half:]
    return jnp.concatenate([x1 * c - x2 * s, x2 * c + x1 * s], axis=-1)


def differential_attention(q, k, v, lam):
    b, s, h, _, d = q.shape
    nqb = s // Q_BLOCK
    scale = d ** -0.5
    qb = q.reshape(b, nqb, Q_BLOCK, h, 2, d).transpose(1, 0, 2, 3, 4, 5)

    def block(q_blk):
        sc = jnp.einsum('bqhcd,bkhcd->bhcqk', q_blk, k).astype(jnp.float32) * scale
        p = jax.nn.softmax(sc, axis=-1)
        a = (p[:, :, 0] - lam * p[:, :, 1]).astype(v.dtype)
        return jnp.einsum('bhqk,bkhe->bqhe', a, v)

    out = lax.map(block, qb)
    return out.transpose(1, 0, 2, 3, 4).reshape(b, s, h, v.shape[-1])


def retention_one_direction(q, k, v, log_gamma, include_diag):
    b, s, h, dk = q.shape
    dv = v.shape[-1]
    c = RET_CHUNK
    n = s // c
    dt = q.dtype
    idx = jnp.arange(c, dtype=jnp.float32)
    diff = idx[:, None] - idx[None, :]
    mask = diff >= 0 if include_diag else diff > 0
    dmat = jnp.where(mask[None], jnp.exp(jnp.where(mask, diff, 0.0)[None] * log_gamma[:, None, None]), 0.0).astype(dt)
    zeta = jnp.exp((c - 1 - idx)[:, None] * log_gamma[None, :]).astype(dt)
    xi = jnp.exp((idx + 1)[:, None] * log_gamma[None, :]).astype(dt)
    g_chunk = jnp.exp(c * log_gamma).astype(dt)

    def to_chunks(t):
        return t.reshape(b, n, c, h, t.shape[-1]).transpose(1, 0, 2, 3, 4)

    def step(state, inp):
        qc, kc, vc = inp
        sc = jnp.einsum('bihd,bjhd->bhij', qc, kc) * dmat[None]
        inner = jnp.einsum('bhij,bjhe->bihe', sc, vc)
        cross = jnp.einsum('bihd,bhde->bihe', qc, state) * xi[None, :, :, None]
        new_state = g_chunk[None, :, None, None] * state + jnp.einsum(
            'bjhd,bjhe->bhde', kc * zeta[None, :, :, None], vc)
        return new_state, inner + cross

    state0 = jnp.zeros((b, h, dk, dv), dt)
    _, ys = lax.scan(step, state0, (to_chunks(q), to_chunks(k), to_chunks(v)))
    return ys.transpose(1, 0, 2, 3, 4).reshape(b, s, h, dv)


def bidirectional_retention(q, k, v, logit_fwd, logit_bwd):
    lg_f = jax.nn.log_sigmoid(logit_fwd.astype(jnp.float32))
    lg_b = jax.nn.log_sigmoid(logit_bwd.astype(jnp.float32))
    fwd = retention_one_direction(q, k, v, lg_f, True)
    flip = lambda t: jnp.flip(t, axis=1)
    bwd = flip(retention_one_direction(flip(q), flip(k), flip(v), lg_b, False))
    return fwd + bwd


def setup_inputs(seed: int = 0) -> dict:
    key = jax.random.key(seed)
    ks = jax.random.split(key, 20)
    f32 = jnp.float32

    def nrm(k, shape, fan_in):
        return jax.random.normal(k, shape, f32) * (fan_in ** -0.5)

    def gain(k, shape):
        return 1.0 + 0.02 * jax.random.normal(k, shape, f32)

    p = jnp.exp2(-5.0 - jnp.arange(RET_HEADS, dtype=f32))
    base_logit = jnp.log1p(-p) - jnp.log(p)
    return {
        "x": jax.random.normal(ks[0], (BATCH, SEQ, D_MODEL), f32),
        "g_mix": gain(ks[1], (DEPTH, D_MODEL)),
        "w_in": nrm(ks[2], (DEPTH, D_MODEL, IN_COLS), D_MODEL),
        "diff_lq1": 0.1 * jax.random.normal(ks[3], (DEPTH, DIFF_HEAD_DIM), f32),
        "diff_lk1": 0.1 * jax.random.normal(ks[4], (DEPTH, DIFF_HEAD_DIM), f32),
        "diff_lq2": 0.1 * jax.random.normal(ks[5], (DEPTH, DIFF_HEAD_DIM), f32),
        "diff_lk2": 0.1 * jax.random.normal(ks[6], (DEPTH, DIFF_HEAD_DIM), f32),
        "diff_subln_g": gain(ks[7], (DEPTH, DIFF_V_DIM)),
        "ret_decay_fwd": base_logit[None] + 0.1 * jax.random.normal(ks[8], (DEPTH, RET_HEADS), f32),
        "ret_decay_bwd": base_logit[None] + 0.1 * jax.random.normal(ks[9], (DEPTH, RET_HEADS), f32),
        "w_up_diff": nrm(ks[10], (DEPTH, DIFF_V_W, D_MODEL), DIFF_V_W),
        "w_up_ret": nrm(ks[11], (DEPTH, RET_V_W, D_MODEL), RET_V_W),
        "w_o": nrm(ks[12], (DEPTH, D_MODEL, D_MODEL), D_MODEL),
        "g_ffn": gain(ks[13], (DEPTH, D_MODEL)),
        "w_ffn_gate": nrm(ks[14], (DEPTH, D_MODEL, D_FF), D_MODEL),
        "w_ffn_up": nrm(ks[15], (DEPTH, D_MODEL, D_FF), D_MODEL),
        "w_ffn_down": nrm(ks[16], (DEPTH, D_FF, D_MODEL), D_FF),
        "g_final": gain(ks[17], (D_MODEL,)),
    }


def reference(x, g_mix, w_in, diff_lq1, diff_lk1, diff_lq2, diff_lk2, diff_subln_g,
              ret_decay_fwd, ret_decay_bwd, w_up_diff, w_up_ret, w_o,
              g_ffn, w_ffn_gate, w_ffn_up, w_ffn_down, g_final):
    b, s, _ = x.shape
    cos, sin = rope_tables(s, DIFF_HEAD_DIM)
    split_pts = list(np.cumsum(IN_SPLITS)[:-1])
    for l in range(DEPTH):
        lam_init = 0.8 - 0.6 * math.exp(-0.3 * l)
        h = rmsnorm(x, g_mix[l])
        proj = h @ w_in[l]
        dq, dk, dv, rq, rk, rv, rg, gate_a, gate_b = jnp.split(proj, split_pts, axis=-1)

        dq = apply_rope(dq.reshape(b, s, DIFF_HEADS, 2, DIFF_HEAD_DIM), cos, sin)
        dk = apply_rope(dk.reshape(b, s, DIFF_HEADS, 2, DIFF_HEAD_DIM), cos, sin)
        dv = dv.reshape(b, s, DIFF_HEADS, DIFF_V_DIM)
        lam = (jnp.exp(jnp.sum(diff_lq1[l].astype(jnp.float32) * diff_lk1[l].astype(jnp.float32)))
               - jnp.exp(jnp.sum(diff_lq2[l].astype(jnp.float32) * diff_lk2[l].astype(jnp.float32)))
               + lam_init)
        a = differential_attention(dq, dk, dv, lam)
        a = rmsnorm(a, diff_subln_g[l]) * (1.0 - lam_init)
        ya = a.reshape(b, s, DIFF_V_W) @ w_up_diff[l]

        rq = apply_rope(rq.reshape(b, s, RET_HEADS, RET_QK_DIM), cos, sin)
        rk = apply_rope(rk.reshape(b, s, RET_HEADS, RET_QK_DIM), cos, sin) * (RET_QK_DIM ** -0.5)
        rv = rv.reshape(b, s, RET_HEADS, RET_V_DIM)
        r = bidirectional_retention(rq, rk, rv, ret_decay_fwd[l], ret_decay_bwd[l])
        r = rmsnorm(r).reshape(b, s, RET_V_W) * jax.nn.silu(rg)
        yb = r @ w_up_ret[l]

        m = jax.nn.sigmoid(gate_a) * ya + jax.nn.sigmoid(gate_b) * yb
        x = x + m @ w_o[l]

        h2 = rmsnorm(x, g_ffn[l])
        x = x + (jax.nn.silu(h2 @ w_ffn_gate[l]) * (h2 @ w_ffn_up[l])) @ w_ffn_down[l]
    return rmsnorm(x, g_final)
```

```python
import functools
import math

import jax
import jax.numpy as jnp
from jax import lax
from jax.experimental import pallas as pl
from jax.experimental.pallas import tpu as pltpu

F32 = jnp.float32
BF16 = jnp.bfloat16

D_MODEL = 1024
HEADS = 4
QK_DIM = 64
V_DIM = 128
DIFF_W = HEADS * 2 * QK_DIM
RET_QK_W = HEADS * QK_DIM
RET_V_W = HEADS * V_DIM
IN_COLS = 3 * DIFF_W + 2 * RET_QK_W + 2 * RET_V_W + 2 * D_MODEL
D_FF = 2816
ROPE_THETA = 10000.0
NORM_EPS = 1e-5
LAM_INIT = 0.8 - 0.6 * math.exp(-0.3 * 0)
LOG2E = 1.4426950408889634

LANES = 128
VMEM_LIMIT = 56 * 1024 * 1024
PROJ_TM = 512
ATTN_TQ = 256
RET_CHUNK = 256
MERGE_TM = 512
FFN_TM = 512


def _rms(x, eps=NORM_EPS):
    return x * lax.rsqrt(jnp.mean(x * x, axis=-1, keepdims=True) + eps)


def _const_spec(shape):
    nd = len(shape)
    return pl.BlockSpec(shape, lambda *_: (0,) * nd, pipeline_mode=pl.Buffered(1))


def _proj_kernel(x_ref, g_ref, w_ref, cos_ref, sin_ref,
                 dq_ref, dk_ref, dv_ref, rq_ref, rk_ref, rv_ref, rg_ref, ga_ref, gb_ref):
    tm = x_ref.shape[0]
    hb = (_rms(x_ref[...]) * g_ref[...]).astype(BF16)
    cos = cos_ref[...]
    sin = sin_ref[...]
    lane = lax.broadcasted_iota(jnp.int32, (tm, LANES), 1)
    first_half = (lane % QK_DIM) < (QK_DIM // 2)

    def proj(c0, n):
        return jnp.dot(hb, w_ref[:, c0:c0 + n], preferred_element_type=F32)

    def rope(y, scale):
        partner = jnp.where(first_half, pltpu.roll(y, LANES - 32, 1), pltpu.roll(y, 32, 1))
        return ((y * cos + partner * sin) * scale).astype(BF16)

    q_scale = QK_DIM ** -0.5 * LOG2E
    y = proj(0, DIFF_W)
    for h in range(HEADS):
        dq_ref[h] = rope(y[:, h * LANES:(h + 1) * LANES], q_scale)
    y = proj(DIFF_W, DIFF_W)
    for h in range(HEADS):
        dk_ref[h] = rope(y[:, h * LANES:(h + 1) * LANES], 1.0)
    y = proj(2 * DIFF_W, DIFF_W)
    for h in range(HEADS):
        dv_ref[h] = y[:, h * LANES:(h + 1) * LANES].astype(BF16)
    c0 = 3 * DIFF_W
    y = proj(c0, 2 * RET_QK_W)
    for p in range(HEADS // 2):
        rq_ref[p] = rope(y[:, p * LANES:(p + 1) * LANES], 1.0)
        rk_ref[p] = rope(y[:, RET_QK_W + p * LANES:RET_QK_W + (p + 1) * LANES], QK_DIM ** -0.5)
    c0 += 2 * RET_QK_W
    y = proj(c0, RET_V_W)
    for p in range(HEADS // 2):
        rv_ref[p] = y[:, p * 2 * V_DIM:(p + 1) * 2 * V_DIM].astype(BF16)
    c0 += RET_V_W
    y = proj(c0, RET_V_W)
    for p in range(HEADS // 2):
        rg_ref[p] = y[:, p * 2 * V_DIM:(p + 1) * 2 * V_DIM].astype(BF16)
    c0 += RET_V_W
    for half in range(2):
        ga_ref[:, half * 512:(half + 1) * 512] = proj(c0 + half * 512, 512).astype(BF16)
    c0 += D_MODEL
    for half in range(2):
        gb_ref[:, half * 512:(half + 1) * 512] = proj(c0 + half * 512, 512).astype(BF16)


def _proj(x2d, g_mix, w_in, cos_t, sin_t, batch, seq):
    tm = PROJ_TM
    nb = seq // tm
    tokens = batch * seq

    def head_major(n_groups, width):
        shape = jax.ShapeDtypeStruct((batch, n_groups, seq, width), BF16)
        spec = pl.BlockSpec((None, n_groups, tm, width), lambda i: (i // nb, 0, i % nb, 0))
        return shape, spec

    outs = [head_major(HEADS, LANES)] * 3 + [head_major(HEADS // 2, LANES)] * 2 \
        + [head_major(HEADS // 2, 2 * V_DIM)] * 2
    out_shape = [o[0] for o in outs] + [jax.ShapeDtypeStruct((tokens, D_MODEL), BF16)] * 2
    out_specs = [o[1] for o in outs] + [pl.BlockSpec((tm, D_MODEL), lambda i: (i, 0))] * 2
    return pl.pallas_call(
        _proj_kernel,
        grid=(tokens // tm,),
        in_specs=[
            pl.BlockSpec((tm, D_MODEL), lambda i: (i, 0)),
            _const_spec((1, D_MODEL)),
            _const_spec((D_MODEL, IN_COLS)),
            pl.BlockSpec((tm, LANES), lambda i: (i % nb, 0)),
            pl.BlockSpec((tm, LANES), lambda i: (i % nb, 0)),
        ],
        out_specs=out_specs,
        out_shape=out_shape,
        compiler_params=pltpu.CompilerParams(
            dimension_semantics=("parallel",), vmem_limit_bytes=VMEM_LIMIT),
        name="in_proj",
    )(x2d, g_mix, w_in, cos_t, sin_t)


def _attn_kernel(lam_ref, g_ref, q_ref, k_ref, v_ref, o_ref, vext_ref):
    @pl.when(pl.program_id(2) == 0)
    def _():
        vext_ref[:, :V_DIM] = v_ref[...]
        vext_ref[:, V_DIM:] = jnp.ones((v_ref.shape[0], V_DIM), BF16)

    lp = lam_ref[...]
    lam = (jnp.exp(jnp.sum(lp[0:1] * lp[1:2], axis=-1, keepdims=True))
           - jnp.exp(jnp.sum(lp[2:3] * lp[3:4], axis=-1, keepdims=True)) + LAM_INIT)

    q = q_ref[...]
    k = k_ref[...]
    lane = lax.broadcasted_iota(jnp.int32, q.shape, 1)
    outs = []
    for c in range(2):
        in_comp = (lane < QK_DIM) if c == 0 else (lane >= QK_DIM)
        qc = jnp.where(in_comp, q, jnp.zeros_like(q))
        s = lax.dot_general(qc, k, (((1,), (1,)), ((), ())), preferred_element_type=F32)
        m = jnp.max(s, axis=-1, keepdims=True)
        p = jnp.exp2(s - m).astype(BF16)
        oe = jnp.dot(p, vext_ref[...], preferred_element_type=F32)
        outs.append(oe[:, :V_DIM] / oe[:, V_DIM:])
    o = outs[0] - lam * outs[1]
    o = _rms(o) * g_ref[...] * (1.0 - LAM_INIT)
    o_ref[...] = o.astype(BF16)


def _attn(lam_params, g_sub, dq, dk, dv):
    batch, heads, seq, _ = dq.shape
    tq = ATTN_TQ
    nq = seq // tq
    return pl.pallas_call(
        _attn_kernel,
        grid=(batch, heads, nq),
        in_specs=[
            _const_spec((4, QK_DIM)),
            _const_spec((1, V_DIM)),
            pl.BlockSpec((None, None, tq, LANES), lambda b, h, i: (b, h, i, 0)),
            pl.BlockSpec((None, None, seq, LANES), lambda b, h, i: (b, h, 0, 0)),
            pl.BlockSpec((None, None, seq, LANES), lambda b, h, i: (b, h, 0, 0)),
        ],
        out_specs=pl.BlockSpec((tq, V_DIM), lambda b, h, i: (b * nq + i, h)),
        out_shape=jax.ShapeDtypeStruct((batch * seq, HEADS * V_DIM), BF16),
        scratch_shapes=[pltpu.VMEM((seq, 2 * V_DIM), BF16)],
        compiler_params=pltpu.CompilerParams(
            dimension_semantics=("parallel", "parallel", "arbitrary"), vmem_limit_bytes=VMEM_LIMIT),
        name="diff_attn",
    )(lam_params, g_sub, dq, dk, dv)


def _log_sigmoid(x):
    return jnp.minimum(x, 0.0) - jnp.log1p(jnp.exp(-jnp.abs(x)))


def _ret_kernel(dec_ref, q_ref, k_ref, v_ref, rg_ref, o_ref, acc_ref):
    seq = q_ref.shape[0]
    c_len = RET_CHUNK
    n_chunks = seq // c_len
    nt_dims = (((1,), (1,)), ((), ()))
    tn_dims = (((0,), (0,)), ((), ()))
    lane = lax.broadcasted_iota(jnp.int32, (c_len, LANES), 1)
    row_i = lax.broadcasted_iota(jnp.int32, (c_len, c_len), 0)
    col_i = lax.broadcasted_iota(jnp.int32, (c_len, c_len), 1)
    diff = (row_i - col_i).astype(F32)
    rows = lax.broadcasted_iota(jnp.int32, (c_len, LANES), 0).astype(F32)

    for hh in range(2):
        lg_f = _log_sigmoid(dec_ref[hh, 0:1, :])
        lg_b = _log_sigmoid(dec_ref[hh, 1:2, :])
        dmat = jnp.exp(jnp.where(diff >= 0, diff * lg_f, -diff * lg_b))
        lg_f1 = lg_f[:, :LANES]
        lg_b1 = lg_b[:, :LANES]
        zeta_f = jnp.exp((c_len - 1 - rows) * lg_f1)
        xi_f = jnp.exp((rows + 1) * lg_f1)
        zeta_b = jnp.exp(rows * lg_b1)
        xi_b = jnp.exp((c_len - rows) * lg_b1)
        g_f = jnp.exp(c_len * lg_f1)
        g_b = jnp.exp(c_len * lg_b1)
        in_head = (lane >= hh * QK_DIM) & (lane < (hh + 1) * QK_DIM)
        vcols = slice(hh * V_DIM, (hh + 1) * V_DIM)

        def load(c):
            r0 = pl.multiple_of(c * c_len, c_len)
            q = q_ref[pl.ds(r0, c_len), :]
            qc = jnp.where(in_head, q, jnp.zeros_like(q))
            return r0, qc, k_ref[pl.ds(r0, c_len), :], v_ref[pl.ds(r0, c_len), vcols]

        def fwd(c, state):
            r0, qc, kc, vc = load(c)
            sc = lax.dot_general(qc, kc, nt_dims, preferred_element_type=F32) * dmat
            inner = jnp.dot(sc.astype(BF16), vc, preferred_element_type=F32)
            cross = jnp.dot(qc, state.astype(BF16), preferred_element_type=F32) * xi_f
            acc_ref[pl.ds(r0, c_len), vcols] = inner + cross
            kz = (kc.astype(F32) * zeta_f).astype(BF16)
            return g_f * state + lax.dot_general(kz, vc, tn_dims, preferred_element_type=F32)

        lax.fori_loop(0, n_chunks, fwd, jnp.zeros((LANES, V_DIM), F32))

        def bwd(t, state):
            c = n_chunks - 1 - t
            r0, qc, kc, vc = load(c)
            cross = jnp.dot(qc, state.astype(BF16), preferred_element_type=F32) * xi_b
            tot = _rms(acc_ref[pl.ds(r0, c_len), vcols] + cross)
            gate = rg_ref[pl.ds(r0, c_len), vcols].astype(F32)
            o_ref[pl.ds(r0, c_len), vcols] = (tot * gate * jax.nn.sigmoid(gate)).astype(BF16)
            kz = (kc.astype(F32) * zeta_b).astype(BF16)
            return g_b * state + lax.dot_general(kz, vc, tn_dims, preferred_element_type=F32)

        lax.fori_loop(0, n_chunks, bwd, jnp.zeros((LANES, V_DIM), F32))


def _ret(dec, rq, rk, rv, rg):
    batch, pairs, seq, _ = rq.shape
    pair_w = 2 * V_DIM
    return pl.pallas_call(
        _ret_kernel,
        grid=(batch, pairs),
        in_specs=[
            pl.BlockSpec((2, 2, RET_CHUNK), lambda b, p: (p, 0, 0)),
            pl.BlockSpec((None, None, seq, LANES), lambda b, p: (b, p, 0, 0)),
            pl.BlockSpec((None, None, seq, LANES), lambda b, p: (b, p, 0, 0)),
            pl.BlockSpec((None, None, seq, pair_w), lambda b, p: (b, p, 0, 0)),
            pl.BlockSpec((None, None, seq, pair_w), lambda b, p: (b, p, 0, 0)),
        ],
        out_specs=pl.BlockSpec((seq, pair_w), lambda b, p: (b, p)),
        out_shape=jax.ShapeDtypeStruct((batch * seq, RET_V_W), BF16),
        scratch_shapes=[pltpu.VMEM((seq, pair_w), F32)],
        compiler_params=pltpu.CompilerParams(
            dimension_semantics=("parallel", "parallel"), vmem_limit_bytes=VMEM_LIMIT),
        name="retention",
    )(dec, rq, rk, rv, rg)


def _merge_kernel(x_ref, a_ref, r_ref, ga_ref, gb_ref, wa_ref, wr_ref, wo_ref, o_ref):
    ya = jnp.dot(a_ref[...], wa_ref[...], preferred_element_type=F32)
    yb = jnp.dot(r_ref[...], wr_ref[...], preferred_element_type=F32)
    m = (jax.nn.sigmoid(ga_ref[...].astype(F32)) * ya
         + jax.nn.sigmoid(gb_ref[...].astype(F32)) * yb)
    o_ref[...] = x_ref[...] + jnp.dot(m.astype(BF16), wo_ref[...], preferred_element_type=F32)


def _merge(x2d, a, r, ga, gb, wa, wr, wo):
    tokens = x2d.shape[0]
    tm = MERGE_TM
    row = lambda w: pl.BlockSpec((tm, w), lambda i: (i, 0))
    return pl.pallas_call(
        _merge_kernel,
        grid=(tokens // tm,),
        in_specs=[row(D_MODEL), row(DIFF_W), row(RET_V_W), row(D_MODEL), row(D_MODEL),
                  _const_spec(wa.shape), _const_spec(wr.shape), _const_spec(wo.shape)],
        out_specs=row(D_MODEL),
        out_shape=jax.ShapeDtypeStruct((tokens, D_MODEL), F32),
        compiler_params=pltpu.CompilerParams(
            dimension_semantics=("parallel",), vmem_limit_bytes=VMEM_LIMIT),
        name="merge",
    )(x2d, a, r, ga, gb, wa, wr, wo)


def _ffn_kernel(x_ref, g2_ref, gf_ref, wg_ref, wu_ref, wd_ref, o_ref):
    x = x_ref[...]
    h = (_rms(x) * g2_ref[...]).astype(BF16)
    gate = jnp.dot(h, wg_ref[...], preferred_element_type=F32)
    up = jnp.dot(h, wu_ref[...], preferred_element_type=F32)
    act = (gate * jax.nn.sigmoid(gate) * up).astype(BF16)
    x = x + jnp.dot(act, wd_ref[...], preferred_element_type=F32)
    o_ref[...] = _rms(x) * gf_ref[...]


def _ffn(x1, g_ffn, g_final, wg, wu, wd):
    tokens = x1.shape[0]
    tm = FFN_TM
    row = pl.BlockSpec((tm, D_MODEL), lambda i: (i, 0))
    return pl.pallas_call(
        _ffn_kernel,
        grid=(tokens // tm,),
        in_specs=[row, _const_spec((1, D_MODEL)), _const_spec((1, D_MODEL)),
                  _const_spec(wg.shape), _const_spec(wu.shape), _const_spec(wd.shape)],
        out_specs=row,
        out_shape=jax.ShapeDtypeStruct((tokens, D_MODEL), F32),
        compiler_params=pltpu.CompilerParams(
            dimension_semantics=("parallel",), vmem_limit_bytes=VMEM_LIMIT),
        name="ffn",
    )(x1, g_ffn, g_final, wg, wu, wd)


def _rope_tables(seq):
    half = QK_DIM // 2
    inv_freq = ROPE_THETA ** (-jnp.arange(0, QK_DIM, 2, dtype=F32) / QK_DIM)
    ang = jnp.arange(seq, dtype=F32)[:, None] * inv_freq[None, :]
    cos, sin = jnp.cos(ang), jnp.sin(ang)
    cos_t = jnp.tile(cos, (1, LANES // half))
    sin_t = jnp.tile(jnp.concatenate([-sin, sin], axis=-1), (1, LANES // QK_DIM))
    return cos_t, sin_t


def kernel(x, g_mix, w_in, diff_lq1, diff_lk1, diff_lq2, diff_lk2, diff_subln_g,
           ret_decay_fwd, ret_decay_bwd, w_up_diff, w_up_ret, w_o,
           g_ffn, w_ffn_gate, w_ffn_up, w_ffn_down, g_final):
    batch, seq, _ = x.shape
    layer = 0
    x2d = x.reshape(batch * seq, D_MODEL)
    cos_t, sin_t = _rope_tables(seq)

    dq, dk, dv, rq, rk, rv, rg, ga, gb = _proj(
        x2d, g_mix[layer][None, :], w_in[layer].astype(BF16), cos_t, sin_t, batch, seq)

    lam_params = jnp.stack([diff_lq1[layer], diff_lk1[layer], diff_lq2[layer], diff_lk2[layer]])
    a = _attn(lam_params.astype(F32), diff_subln_g[layer][None, :].astype(F32), dq, dk, dv)

    dec = jnp.stack([ret_decay_fwd[layer], ret_decay_bwd[layer]], axis=1).astype(F32)
    dec = jnp.broadcast_to(dec[:, :, None], (HEADS, 2, RET_CHUNK))
    r = _ret(dec, rq, rk, rv, rg)

    x1 = _merge(x2d, a, r, ga, gb, w_up_diff[layer].astype(BF16), w_up_ret[layer].astype(BF16),
                w_o[layer].astype(BF16))
    y = _ffn(x1, g_ffn[layer][None, :], g_final[None, :], w_ffn_gate[layer].astype(BF16),
             w_ffn_up[layer].astype(BF16), w_ffn_down[layer].astype(BF16))
    return y.reshape(batch, seq, D_MODEL)
```

```python
import functools
import math

import jax
import jax.numpy as jnp
from jax import lax
from jax.experimental import pallas as pl
from jax.experimental.pallas import tpu as pltpu

F32 = jnp.float32
BF16 = jnp.bfloat16

D_MODEL = 1024
HEADS = 4
QK_DIM = 64
V_DIM = 128
DIFF_W = HEADS * 2 * QK_DIM
RET_QK_W = HEADS * QK_DIM
RET_V_W = HEADS * V_DIM
IN_COLS = 3 * DIFF_W + 2 * RET_QK_W + 2 * RET_V_W + 2 * D_MODEL
D_FF = 2816
ROPE_THETA = 10000.0
NORM_EPS = 1e-5
LAM_INIT = 0.8 - 0.6 * math.exp(-0.3 * 0)
LOG2E = 1.4426950408889634

LANES = 128
VMEM_LIMIT = 56 * 1024 * 1024
PROJ_TM = 512
ATTN_TQ = 256
ATTN_ONES_ROWS = 16
ATTN_KB = 512
RET_CHUNK = 256
MERGE_TM = 512
FFN_TM = 512


def _rms(x, eps=NORM_EPS):
    return x * lax.rsqrt(jnp.mean(x * x, axis=-1, keepdims=True) + eps)


def _const_spec(shape):
    nd = len(shape)
    return pl.BlockSpec(shape, lambda *_: (0,) * nd, pipeline_mode=pl.Buffered(1))


def _proj_kernel(x_ref, g_ref, w_ref, cos_ref, sin_ref,
                 dq_ref, dk_ref, dv_ref, rq_ref, rk_ref, rv_ref, rg_ref, ga_ref, gb_ref):
    tm = x_ref.shape[0]
    hb = (_rms(x_ref[...]) * g_ref[...]).astype(BF16)
    cos = cos_ref[...]
    sin = sin_ref[...]
    lane = lax.broadcasted_iota(jnp.int32, (tm, LANES), 1)
    first_half = (lane % QK_DIM) < (QK_DIM // 2)

    def proj(c0, n):
        return jnp.dot(hb, w_ref[:, c0:c0 + n], preferred_element_type=F32)

    def rope(y, scale):
        partner = jnp.where(first_half, pltpu.roll(y, LANES - 32, 1), pltpu.roll(y, 32, 1))
        return ((y * cos + partner * sin) * scale).astype(BF16)

    q_scale = QK_DIM ** -0.5 * LOG2E
    y = proj(0, DIFF_W)
    for h in range(HEADS):
        dq_ref[h] = rope(y[:, h * LANES:(h + 1) * LANES], q_scale)
    y = proj(DIFF_W, DIFF_W)
    for h in range(HEADS):
        dk_ref[h] = rope(y[:, h * LANES:(h + 1) * LANES], 1.0)
    y = proj(2 * DIFF_W, DIFF_W)
    for h in range(HEADS):
        dv_ref[h] = y[:, h * LANES:(h + 1) * LANES].astype(BF16)
    c0 = 3 * DIFF_W
    y = proj(c0, 2 * RET_QK_W)
    for p in range(HEADS // 2):
        rq_ref[p] = rope(y[:, p * LANES:(p + 1) * LANES], 1.0)
        rk_ref[p] = rope(y[:, RET_QK_W + p * LANES:RET_QK_W + (p + 1) * LANES], QK_DIM ** -0.5)
    c0 += 2 * RET_QK_W
    y = proj(c0, RET_V_W)
    for p in range(HEADS // 2):
        rv_ref[p] = y[:, p * 2 * V_DIM:(p + 1) * 2 * V_DIM].astype(BF16)
    c0 += RET_V_W
    y = proj(c0, RET_V_W)
    for p in range(HEADS // 2):
        rg_ref[p] = y[:, p * 2 * V_DIM:(p + 1) * 2 * V_DIM].astype(BF16)
    c0 += RET_V_W
    for half in range(2):
        ga_ref[:, half * 512:(half + 1) * 512] = proj(c0 + half * 512, 512).astype(BF16)
    c0 += D_MODEL
    for half in range(2):
        gb_ref[:, half * 512:(half + 1) * 512] = proj(c0 + half * 512, 512).astype(BF16)


def _proj(x2d, g_mix, w_in, cos_t, sin_t, batch, seq):
    tm = PROJ_TM
    nb = seq // tm
    tokens = batch * seq

    def head_major(n_groups, width):
        shape = jax.ShapeDtypeStruct((batch, n_groups, seq, width), BF16)
        spec = pl.BlockSpec((None, n_groups, tm, width), lambda i: (i // nb, 0, i % nb, 0))
        return shape, spec

    outs = [head_major(HEADS, LANES)] * 3 + [head_major(HEADS // 2, LANES)] * 2 \
        + [head_major(HEADS // 2, 2 * V_DIM)] * 2
    out_shape = [o[0] for o in outs] + [jax.ShapeDtypeStruct((tokens, D_MODEL), BF16)] * 2
    out_specs = [o[1] for o in outs] + [pl.BlockSpec((tm, D_MODEL), lambda i: (i, 0))] * 2
    return pl.pallas_call(
        _proj_kernel,
        grid=(tokens // tm,),
        in_specs=[
            pl.BlockSpec((tm, D_MODEL), lambda i: (i, 0)),
            _const_spec((1, D_MODEL)),
            _const_spec((D_MODEL, IN_COLS)),
            pl.BlockSpec((tm, LANES), lambda i: (i % nb, 0)),
            pl.BlockSpec((tm, LANES), lambda i: (i % nb, 0)),
        ],
        out_specs=out_specs,
        out_shape=out_shape,
        compiler_params=pltpu.CompilerParams(
            dimension_semantics=("parallel",), vmem_limit_bytes=VMEM_LIMIT),
        name="in_proj",
    )(x2d, g_mix, w_in, cos_t, sin_t)


def _attn_kernel(lam_ref, g_ref, q_ref, k_ref, v_ref, o_ref,
                 vext_ref, s0_ref, s1_ref, m0_ref, m1_ref, p0_ref, p1_ref):
    seq = q_ref.shape[0]
    tq = ATTN_TQ
    nq = seq // tq
    nt_dims = (((1,), (1,)), ((), ()))

    vext_ref[:V_DIM, :] = v_ref[...].astype(F32).T.astype(BF16)
    vext_ref[V_DIM:, :] = jnp.ones((ATTN_ONES_ROWS, seq), BF16)

    lp = lam_ref[...]
    lam = (jnp.exp(jnp.sum(lp[0:1] * lp[1:2], axis=-1, keepdims=True))
           - jnp.exp(jnp.sum(lp[2:3] * lp[3:4], axis=-1, keepdims=True)) + LAM_INIT)
    lane = lax.broadcasted_iota(jnp.int32, (tq, LANES), 1)

    bufs = ((s0_ref, m0_ref, p0_ref), (s1_ref, m1_ref, p1_ref))

    def step(qk=None, sm=None, pv=None):
        if qk is not None:
            i, (qs_ref, qm_ref, _) = qk
            q = q_ref[pl.ds(pl.multiple_of(i * tq, tq), tq), :]
            zero = jnp.zeros_like(q)
            q2 = jnp.concatenate([jnp.where(lane < QK_DIM, q, zero),
                                  jnp.where(lane >= QK_DIM, q, zero)], axis=0)
            m = None
        if sm is not None:
            _, (ss_ref, sm_ref, sp_ref) = sm
            m_prev = sm_ref[...]
        if pv is not None:
            j, (_, _, pp_ref) = pv
            acc = None
        n_kb = seq // ATTN_KB
        for kb in range(n_kb):
            rows = slice(kb * ATTN_KB, (kb + 1) * ATTN_KB)
            if qk is not None:
                s = lax.dot_general(k_ref[rows, :], q2, nt_dims, preferred_element_type=F32)
                qs_ref[rows, :] = s
                blk = jnp.max(s, axis=0, keepdims=True)
                m = blk if m is None else jnp.maximum(m, blk)
            if sm is not None:
                sp_ref[rows, :] = jnp.exp2(ss_ref[rows, :] - m_prev).astype(BF16)
            if pv is not None and 2 * kb < n_kb:
                for kb2 in (2 * kb, 2 * kb + 1):
                    rows2 = slice(kb2 * ATTN_KB, (kb2 + 1) * ATTN_KB)
                    part = jnp.dot(vext_ref[:, rows2], pp_ref[rows2, :], preferred_element_type=F32)
                    acc = part if acc is None else acc + part
            if pv is not None and 2 * kb + 2 == n_kb:
                o = acc[:V_DIM] / acc[V_DIM:V_DIM + 1]
                o = (o[:, :tq] - lam * o[:, tq:]).T
                o = _rms(o) * g_ref[...] * (1.0 - LAM_INIT)
                o_ref[pl.ds(pl.multiple_of(j * tq, tq), tq), :] = o.astype(BF16)
        if qk is not None:
            qm_ref[...] = m

    def unit(t, parity):
        return t, bufs[parity]

    step(qk=unit(0, 0))
    step(qk=unit(1, 1), sm=unit(0, 0))

    def body(jj, carry):
        t = 2 * jj
        @pl.when(jj > 0)
        def _():
            step(qk=unit(t, 0), sm=unit(t - 1, 1), pv=unit(t - 2, 0))

        @pl.when(jj < nq)
        def _():
            step(qk=unit(t + 1, 1), sm=unit(t, 0), pv=unit(t - 1, 1))
        return carry

    lax.fori_loop(1, nq // 2, body, 0)

    step(sm=unit(nq - 1, 1), pv=unit(nq - 2, 0))
    step(pv=unit(nq - 1, 1))


def _attn(lam_params, g_sub, dq, dk, dv):
    batch, heads, seq, _ = dq.shape
    tq = ATTN_TQ
    assert (seq // tq) % 2 == 0
    seq_spec = pl.BlockSpec((None, None, seq, LANES), lambda b, h: (b, h, 0, 0))
    return pl.pallas_call(
        _attn_kernel,
        grid=(batch, heads),
        in_specs=[_const_spec((4, QK_DIM)), _const_spec((1, V_DIM)), seq_spec, seq_spec, seq_spec],
        out_specs=pl.BlockSpec((seq, V_DIM), lambda b, h: (b, h)),
        out_shape=jax.ShapeDtypeStruct((batch * seq, HEADS * V_DIM), BF16),
        scratch_shapes=[
            pltpu.VMEM((V_DIM + ATTN_ONES_ROWS, seq), BF16),
            pltpu.VMEM((seq, 2 * tq), F32), pltpu.VMEM((seq, 2 * tq), F32),
            pltpu.VMEM((1, 2 * tq), F32), pltpu.VMEM((1, 2 * tq), F32),
            pltpu.VMEM((seq, 2 * tq), BF16), pltpu.VMEM((seq, 2 * tq), BF16),
        ],
        compiler_params=pltpu.CompilerParams(
            dimension_semantics=("parallel", "parallel"), vmem_limit_bytes=VMEM_LIMIT),
        name="diff_attn",
    )(lam_params, g_sub, dq, dk, dv)


def _log_sigmoid(x):
    return jnp.minimum(x, 0.0) - jnp.log1p(jnp.exp(-jnp.abs(x)))


def _ret_kernel(dec_ref, q_ref, k_ref, v_ref, rg_ref, o_ref, acc_ref):
    seq = q_ref.shape[0]
    c_len = RET_CHUNK
    n_chunks = seq // c_len
    nt_dims = (((1,), (1,)), ((), ()))
    tn_dims = (((0,), (0,)), ((), ()))
    lane = lax.broadcasted_iota(jnp.int32, (c_len, LANES), 1)
    row_i = lax.broadcasted_iota(jnp.int32, (c_len, c_len), 0)
    col_i = lax.broadcasted_iota(jnp.int32, (c_len, c_len), 1)
    diff = (row_i - col_i).astype(F32)
    rows = lax.broadcasted_iota(jnp.int32, (c_len, LANES), 0).astype(F32)

    for hh in range(2):
        lg_f = _log_sigmoid(dec_ref[hh, 0:1, :])
        lg_b = _log_sigmoid(dec_ref[hh, 1:2, :])
        dmat = jnp.exp(jnp.where(diff >= 0, diff * lg_f, -diff * lg_b))
        lg_f1 = lg_f[:, :LANES]
        lg_b1 = lg_b[:, :LANES]
        zeta_f = jnp.exp((c_len - 1 - rows) * lg_f1)
        xi_f = jnp.exp((rows + 1) * lg_f1)
        zeta_b = jnp.exp(rows * lg_b1)
        xi_b = jnp.exp((c_len - rows) * lg_b1)
        g_f = jnp.exp(c_len * lg_f1)
        g_b = jnp.exp(c_len * lg_b1)
        in_head = (lane >= hh * QK_DIM) & (lane < (hh + 1) * QK_DIM)
        vcols = slice(hh * V_DIM, (hh + 1) * V_DIM)

        def load(c):
            r0 = pl.multiple_of(c * c_len, c_len)
            q = q_ref[pl.ds(r0, c_len), :]
            qc = jnp.where(in_head, q, jnp.zeros_like(q))
            return r0, qc, k_ref[pl.ds(r0, c_len), :], v_ref[pl.ds(r0, c_len), vcols]

        def fwd(c, state):
            r0, qc, kc, vc = load(c)
            sc = lax.dot_general(qc, kc, nt_dims, preferred_element_type=F32) * dmat
            inner = jnp.dot(sc.astype(BF16), vc, preferred_element_type=F32)
            cross = jnp.dot(qc, state.astype(BF16), preferred_element_type=F32) * xi_f
            acc_ref[pl.ds(r0, c_len), vcols] = inner + cross
            kz = (kc.astype(F32) * zeta_f).astype(BF16)
            return g_f * state + lax.dot_general(kz, vc, tn_dims, preferred_element_type=F32)

        lax.fori_loop(0, n_chunks, fwd, jnp.zeros((LANES, V_DIM), F32))

        def bwd(t, state):
            c = n_chunks - 1 - t
            r0, qc, kc, vc = load(c)
            cross = jnp.dot(qc, state.astype(BF16), preferred_element_type=F32) * xi_b
            tot = _rms(acc_ref[pl.ds(r0, c_len), vcols] + cross)
            gate = rg_ref[pl.ds(r0, c_len), vcols].astype(F32)
            o_ref[pl.ds(r0, c_len), vcols] = (tot * gate * jax.nn.sigmoid(gate)).astype(BF16)
            kz = (kc.astype(F32) * zeta_b).astype(BF16)
            return g_b * state + lax.dot_general(kz, vc, tn_dims, preferred_element_type=F32)

        lax.fori_loop(0, n_chunks, bwd, jnp.zeros((LANES, V_DIM), F32))


def _ret(dec, rq, rk, rv, rg):
    batch, pairs, seq, _ = rq.shape
    pair_w = 2 * V_DIM
    return pl.pallas_call(
        _ret_kernel,
        grid=(batch, pairs),
        in_specs=[
            pl.BlockSpec((2, 2, RET_CHUNK), lambda b, p: (p, 0, 0)),
            pl.BlockSpec((None, None, seq, LANES), lambda b, p: (b, p, 0, 0)),
            pl.BlockSpec((None, None, seq, LANES), lambda b, p: (b, p, 0, 0)),
            pl.BlockSpec((None, None, seq, pair_w), lambda b, p: (b, p, 0, 0)),
            pl.BlockSpec((None, None, seq, pair_w), lambda b, p: (b, p, 0, 0)),
        ],
        out_specs=pl.BlockSpec((seq, pair_w), lambda b, p: (b, p)),
        out_shape=jax.ShapeDtypeStruct((batch * seq, RET_V_W), BF16),
        scratch_shapes=[pltpu.VMEM((seq, pair_w), F32)],
        compiler_params=pltpu.CompilerParams(
            dimension_semantics=("parallel", "parallel"), vmem_limit_bytes=VMEM_LIMIT),
        name="retention",
    )(dec, rq, rk, rv, rg)


def _merge_kernel(x_ref, a_ref, r_ref, ga_ref, gb_ref, wa_ref, wr_ref, wo_ref, o_ref):
    ya = jnp.dot(a_ref[...], wa_ref[...], preferred_element_type=F32)
    yb = jnp.dot(r_ref[...], wr_ref[...], preferred_element_type=F32)
    m = (jax.nn.sigmoid(ga_ref[...].astype(F32)) * ya
         + jax.nn.sigmoid(gb_ref[...].astype(F32)) * yb)
    o_ref[...] = x_ref[...] + jnp.dot(m.astype(BF16), wo_ref[...], preferred_element_type=F32)


def _merge(x2d, a, r, ga, gb, wa, wr, wo):
    tokens = x2d.shape[0]
    tm = MERGE_TM
    row = lambda w: pl.BlockSpec((tm, w), lambda i: (i, 0))
    return pl.pallas_call(
        _merge_kernel,
        grid=(tokens // tm,),
        in_specs=[row(D_MODEL), row(DIFF_W), row(RET_V_W), row(D_MODEL), row(D_MODEL),
                  _const_spec(wa.shape), _const_spec(wr.shape), _const_spec(wo.shape)],
        out_specs=row(D_MODEL),
        out_shape=jax.ShapeDtypeStruct((tokens, D_MODEL), F32),
        compiler_params=pltpu.CompilerParams(
            dimension_semantics=("parallel",), vmem_limit_bytes=VMEM_LIMIT),
        name="merge",
    )(x2d, a, r, ga, gb, wa, wr, wo)


def _ffn_kernel(x_ref, g2_ref, gf_ref, wg_ref, wu_ref, wd_ref, o_ref):
    x = x_ref[...]
    h = (_rms(x) * g2_ref[...]).astype(BF16)
    gate = jnp.dot(h, wg_ref[...], preferred_element_type=F32)
    up = jnp.dot(h, wu_ref[...], preferred_element_type=F32)
    act = (gate * jax.nn.sigmoid(gate) * up).astype(BF16)
    x = x + jnp.dot(act, wd_ref[...], preferred_element_type=F32)
    o_ref[...] = _rms(x) * gf_ref[...]


def _ffn(x1, g_ffn, g_final, wg, wu, wd):
    tokens = x1.shape[0]
    tm = FFN_TM
    row = pl.BlockSpec((tm, D_MODEL), lambda i: (i, 0))
    return pl.pallas_call(
        _ffn_kernel,
        grid=(tokens // tm,),
        in_specs=[row, _const_spec((1, D_MODEL)), _const_spec((1, D_MODEL)),
                  _const_spec(wg.shape), _const_spec(wu.shape), _const_spec(wd.shape)],
        out_specs=row,
        out_shape=jax.ShapeDtypeStruct((tokens, D_MODEL), F32),
        compiler_params=pltpu.CompilerParams(
            dimension_semantics=("parallel",), vmem_limit_bytes=VMEM_LIMIT),
        name="ffn",
    )(x1, g_ffn, g_final, wg, wu, wd)


def _rope_tables(seq):
    half = QK_DIM // 2
    inv_freq = ROPE_THETA ** (-jnp.arange(0, QK_DIM, 2, dtype=F32) / QK_DIM)
    ang = jnp.arange(seq, dtype=F32)[:, None] * inv_freq[None, :]
    cos, sin = jnp.cos(ang), jnp.sin(ang)
    cos_t = jnp.tile(cos, (1, LANES // half))
    sin_t = jnp.tile(jnp.concatenate([-sin, sin], axis=-1), (1, LANES // QK_DIM))
    return cos_t, sin_t


def kernel(x, g_mix, w_in, diff_lq1, diff_lk1, diff_lq2, diff_lk2, diff_subln_g,
           ret_decay_fwd, ret_decay_bwd, w_up_diff, w_up_ret, w_o,
           g_ffn, w_ffn_gate, w_ffn_up, w_ffn_down, g_final):
    batch, seq, _ = x.shape
    layer = 0
    x2d = x.reshape(batch * seq, D_MODEL)
    cos_t, sin_t = _rope_tables(seq)

    dq, dk, dv, rq, rk, rv, rg, ga, gb = _proj(
        x2d, g_mix[layer][None, :], w_in[layer].astype(BF16), cos_t, sin_t, batch, seq)

    lam_params = jnp.stack([diff_lq1[layer], diff_lk1[layer], diff_lq2[layer], diff_lk2[layer]])
    a = _attn(lam_params.astype(F32), diff_subln_g[layer][None, :].astype(F32), dq, dk, dv)

    dec = jnp.stack([ret_decay_fwd[layer], ret_decay_bwd[layer]], axis=1).astype(F32)
    dec = jnp.broadcast_to(dec[:, :, None], (HEADS, 2, RET_CHUNK))
    r = _ret(dec, rq, rk, rv, rg)

    x1 = _merge(x2d, a, r, ga, gb, w_up_diff[layer].astype(BF16), w_up_ret[layer].astype(BF16),
                w_o[layer].astype(BF16))
    y = _ffn(x1, g_ffn[layer][None, :], g_final[None, :], w_ffn_gate[layer].astype(BF16),
             w_ffn_up[layer].astype(BF16), w_ffn_down[layer].astype(BF16))
    return y.reshape(batch, seq, D_MODEL)
```

```python
import functools
import math

import jax
import jax.numpy as jnp
from jax import lax
from jax.experimental import pallas as pl
from jax.experimental.pallas import tpu as pltpu

F32 = jnp.float32
BF16 = jnp.bfloat16

D_MODEL = 1024
HEADS = 4
QK_DIM = 64
V_DIM = 128
DIFF_W = HEADS * 2 * QK_DIM
RET_QK_W = HEADS * QK_DIM
RET_V_W = HEADS * V_DIM
IN_COLS = 3 * DIFF_W + 2 * RET_QK_W + 2 * RET_V_W + 2 * D_MODEL
D_FF = 2816
ROPE_THETA = 10000.0
NORM_EPS = 1e-5
LAM_INIT = 0.8 - 0.6 * math.exp(-0.3 * 0)
LOG2E = 1.4426950408889634

LANES = 128
VMEM_LIMIT = 56 * 1024 * 1024
PROJ_TM = 512
ATTN_TQ = 256
ATTN_ONES_ROWS = 16
ATTN_KB = 512
ATTN_PB = 256
RET_CHUNK = 256
MERGE_TM = 512
FFN_TM = 512


def _rms(x, eps=NORM_EPS):
    return x * lax.rsqrt(jnp.mean(x * x, axis=-1, keepdims=True) + eps)


def _const_spec(shape):
    nd = len(shape)
    return pl.BlockSpec(shape, lambda *_: (0,) * nd, pipeline_mode=pl.Buffered(1))


def _proj_kernel(x_ref, g_ref, w_ref, cos_ref, sin_ref,
                 dq_ref, dk_ref, dv_ref, rq_ref, rk_ref, rv_ref, rg_ref, ga_ref, gb_ref):
    tm = x_ref.shape[0]
    hb = (_rms(x_ref[...]) * g_ref[...]).astype(BF16)
    cos = cos_ref[...]
    sin = sin_ref[...]
    lane = lax.broadcasted_iota(jnp.int32, (tm, LANES), 1)
    first_half = (lane % QK_DIM) < (QK_DIM // 2)

    def proj(c0, n):
        return jnp.dot(hb, w_ref[:, c0:c0 + n], preferred_element_type=F32)

    def rope(y, scale):
        partner = jnp.where(first_half, pltpu.roll(y, LANES - 32, 1), pltpu.roll(y, 32, 1))
        return ((y * cos + partner * sin) * scale).astype(BF16)

    q_scale = QK_DIM ** -0.5 * LOG2E
    y = proj(0, DIFF_W)
    for h in range(HEADS):
        dq_ref[h] = rope(y[:, h * LANES:(h + 1) * LANES], q_scale)
    y = proj(DIFF_W, DIFF_W)
    for h in range(HEADS):
        dk_ref[h] = rope(y[:, h * LANES:(h + 1) * LANES], 1.0)
    y = proj(2 * DIFF_W, DIFF_W)
    for h in range(HEADS):
        dv_ref[h] = y[:, h * LANES:(h + 1) * LANES].astype(BF16)
    c0 = 3 * DIFF_W
    y = proj(c0, 2 * RET_QK_W)
    for p in range(HEADS // 2):
        rq_ref[p] = rope(y[:, p * LANES:(p + 1) * LANES], 1.0)
        rk_ref[p] = rope(y[:, RET_QK_W + p * LANES:RET_QK_W + (p + 1) * LANES], QK_DIM ** -0.5)
    c0 += 2 * RET_QK_W
    y = proj(c0, RET_V_W)
    for p in range(HEADS // 2):
        rv_ref[p] = y[:, p * 2 * V_DIM:(p + 1) * 2 * V_DIM].astype(BF16)
    c0 += RET_V_W
    y = proj(c0, RET_V_W)
    for p in range(HEADS // 2):
        rg_ref[p] = y[:, p * 2 * V_DIM:(p + 1) * 2 * V_DIM].astype(BF16)
    c0 += RET_V_W
    for half in range(2):
        ga_ref[:, half * 512:(half + 1) * 512] = proj(c0 + half * 512, 512).astype(BF16)
    c0 += D_MODEL
    for half in range(2):
        gb_ref[:, half * 512:(half + 1) * 512] = proj(c0 + half * 512, 512).astype(BF16)


def _proj(x2d, g_mix, w_in, cos_t, sin_t, batch, seq):
    tm = PROJ_TM
    nb = seq // tm
    tokens = batch * seq

    def head_major(n_groups, width):
        shape = jax.ShapeDtypeStruct((batch, n_groups, seq, width), BF16)
        spec = pl.BlockSpec((None, n_groups, tm, width), lambda i: (i // nb, 0, i % nb, 0))
        return shape, spec

    outs = [head_major(HEADS, LANES)] * 3 + [head_major(HEADS // 2, LANES)] * 2 \
        + [head_major(HEADS // 2, 2 * V_DIM)] * 2
    out_shape = [o[0] for o in outs] + [jax.ShapeDtypeStruct((tokens, D_MODEL), BF16)] * 2
    out_specs = [o[1] for o in outs] + [pl.BlockSpec((tm, D_MODEL), lambda i: (i, 0))] * 2
    return pl.pallas_call(
        _proj_kernel,
        grid=(tokens // tm,),
        in_specs=[
            pl.BlockSpec((tm, D_MODEL), lambda i: (i, 0)),
            _const_spec((1, D_MODEL)),
            _const_spec((D_MODEL, IN_COLS)),
            pl.BlockSpec((tm, LANES), lambda i: (i % nb, 0)),
            pl.BlockSpec((tm, LANES), lambda i: (i % nb, 0)),
        ],
        out_specs=out_specs,
        out_shape=out_shape,
        compiler_params=pltpu.CompilerParams(
            dimension_semantics=("parallel",), vmem_limit_bytes=VMEM_LIMIT),
        name="in_proj",
    )(x2d, g_mix, w_in, cos_t, sin_t)


def _attn_kernel(lam_ref, g_ref, q_ref, k_ref, v_ref, o_ref,
                 vext_ref, s0_ref, s1_ref, m0_ref, m1_ref, a0_ref, a1_ref):
    seq = q_ref.shape[0]
    tq = ATTN_TQ
    nq = seq // tq
    nt_dims = (((1,), (1,)), ((), ()))

    vext_ref[:V_DIM, :] = v_ref[...].astype(F32).T.astype(BF16)
    vext_ref[V_DIM:, :] = jnp.ones((ATTN_ONES_ROWS, seq), BF16)

    lp = lam_ref[...]
    lam = (jnp.exp(jnp.sum(lp[0:1] * lp[1:2], axis=-1, keepdims=True))
           - jnp.exp(jnp.sum(lp[2:3] * lp[3:4], axis=-1, keepdims=True)) + LAM_INIT)
    lane = lax.broadcasted_iota(jnp.int32, (tq, LANES), 1)

    bufs = ((s0_ref, m0_ref, a0_ref), (s1_ref, m1_ref, a1_ref))

    def step(qk=None, pv=None, fin=None):
        if fin is not None:
            j, (_, _, fa_ref) = fin
            acc = fa_ref[...]
            o = acc[:V_DIM] / acc[V_DIM:V_DIM + 1]
            o = (o[:, :tq] - lam * o[:, tq:]).T
            o = _rms(o) * g_ref[...] * (1.0 - LAM_INIT)
            o_ref[pl.ds(pl.multiple_of(j * tq, tq), tq), :] = o.astype(BF16)
        if qk is not None:
            i, (qs_ref, qm_ref, _) = qk
            q = q_ref[pl.ds(pl.multiple_of(i * tq, tq), tq), :]
            zero = jnp.zeros_like(q)
            q2 = jnp.concatenate([jnp.where(lane < QK_DIM, q, zero),
                                  jnp.where(lane >= QK_DIM, q, zero)], axis=0)
            m = None
        if pv is not None:
            _, (ps_ref, pm_ref, pa_ref) = pv
            m_prev = pm_ref[...]
            acc = None
        for kb in range(seq // ATTN_KB):
            rows = slice(kb * ATTN_KB, (kb + 1) * ATTN_KB)
            if qk is not None:
                s = lax.dot_general(k_ref[rows, :], q2, nt_dims, preferred_element_type=F32)
                qs_ref[rows, :] = s
                blk = jnp.max(s, axis=0, keepdims=True)
                m = blk if m is None else jnp.maximum(m, blk)
            if pv is not None:
                for sub in range(ATTN_KB // ATTN_PB):
                    r0 = kb * ATTN_KB + sub * ATTN_PB
                    rows2 = slice(r0, r0 + ATTN_PB)
                    p = jnp.exp2(ps_ref[rows2, :] - m_prev).astype(BF16)
                    part = jnp.dot(vext_ref[:, rows2], p, preferred_element_type=F32)
                    acc = part if acc is None else acc + part
        if qk is not None:
            qm_ref[...] = m
        if pv is not None:
            pa_ref[...] = acc

    def unit(t, parity):
        return t, bufs[parity]

    step(qk=unit(0, 0))
    step(qk=unit(1, 1), pv=unit(0, 0))

    def body(jj, carry):
        t = 2 * jj
        @pl.when(jj > 0)
        def _():
            step(qk=unit(t, 0), pv=unit(t - 1, 1), fin=unit(t - 2, 0))

        @pl.when(jj < nq)
        def _():
            step(qk=unit(t + 1, 1), pv=unit(t, 0), fin=unit(t - 1, 1))
        return carry

    lax.fori_loop(1, nq // 2, body, 0)

    step(pv=unit(nq - 1, 1), fin=unit(nq - 2, 0))
    step(fin=unit(nq - 1, 1))


def _attn(lam_params, g_sub, dq, dk, dv):
    batch, heads, seq, _ = dq.shape
    tq = ATTN_TQ
    assert (seq // tq) % 2 == 0
    seq_spec = pl.BlockSpec((None, None, seq, LANES), lambda b, h: (b, h, 0, 0))
    return pl.pallas_call(
        _attn_kernel,
        grid=(batch, heads),
        in_specs=[_const_spec((4, QK_DIM)), _const_spec((1, V_DIM)), seq_spec, seq_spec, seq_spec],
        out_specs=pl.BlockSpec((seq, V_DIM), lambda b, h: (b, h)),
        out_shape=jax.ShapeDtypeStruct((batch * seq, HEADS * V_DIM), BF16),
        scratch_shapes=[
            pltpu.VMEM((V_DIM + ATTN_ONES_ROWS, seq), BF16),
            pltpu.VMEM((seq, 2 * tq), F32), pltpu.VMEM((seq, 2 * tq), F32),
            pltpu.VMEM((1, 2 * tq), F32), pltpu.VMEM((1, 2 * tq), F32),
            pltpu.VMEM((V_DIM + ATTN_ONES_ROWS, 2 * tq), F32),
            pltpu.VMEM((V_DIM + ATTN_ONES_ROWS, 2 * tq), F32),
        ],
        compiler_params=pltpu.CompilerParams(
            dimension_semantics=("parallel", "parallel"), vmem_limit_bytes=VMEM_LIMIT),
        name="diff_attn",
    )(lam_params, g_sub, dq, dk, dv)


def _log_sigmoid(x):
    return jnp.minimum(x, 0.0) - jnp.log1p(jnp.exp(-jnp.abs(x)))


def _ret_kernel(dec_ref, q_ref, k_ref, v_ref, rg_ref, o_ref, acc_ref):
    seq = q_ref.shape[0]
    c_len = RET_CHUNK
    n_chunks = seq // c_len
    nt_dims = (((1,), (1,)), ((), ()))
    tn_dims = (((0,), (0,)), ((), ()))
    lane = lax.broadcasted_iota(jnp.int32, (c_len, LANES), 1)
    row_i = lax.broadcasted_iota(jnp.int32, (c_len, c_len), 0)
    col_i = lax.broadcasted_iota(jnp.int32, (c_len, c_len), 1)
    diff = (row_i - col_i).astype(F32)
    rows = lax.broadcasted_iota(jnp.int32, (c_len, LANES), 0).astype(F32)

    for hh in range(2):
        lg_f = _log_sigmoid(dec_ref[hh, 0:1, :])
        lg_b = _log_sigmoid(dec_ref[hh, 1:2, :])
        dmat = jnp.exp(jnp.where(diff >= 0, diff * lg_f, -diff * lg_b))
        lg_f1 = lg_f[:, :LANES]
        lg_b1 = lg_b[:, :LANES]
        zeta_f = jnp.exp((c_len - 1 - rows) * lg_f1)
        xi_f = jnp.exp((rows + 1) * lg_f1)
        zeta_b = jnp.exp(rows * lg_b1)
        xi_b = jnp.exp((c_len - rows) * lg_b1)
        g_f = jnp.exp(c_len * lg_f1)
        g_b = jnp.exp(c_len * lg_b1)
        in_head = (lane >= hh * QK_DIM) & (lane < (hh + 1) * QK_DIM)
        vcols = slice(hh * V_DIM, (hh + 1) * V_DIM)

        def load(c):
            r0 = pl.multiple_of(c * c_len, c_len)
            q = q_ref[pl.ds(r0, c_len), :]
            qc = jnp.where(in_head, q, jnp.zeros_like(q))
            return r0, qc, k_ref[pl.ds(r0, c_len), :], v_ref[pl.ds(r0, c_len), vcols]

        def fwd(c, state):
            r0, qc, kc, vc = load(c)
            sc = lax.dot_general(qc, kc, nt_dims, preferred_element_type=F32) * dmat
            inner = jnp.dot(sc.astype(BF16), vc, preferred_element_type=F32)
            cross = jnp.dot(qc, state.astype(BF16), preferred_element_type=F32) * xi_f
            acc_ref[pl.ds(r0, c_len), vcols] = inner + cross
            kz = (kc.astype(F32) * zeta_f).astype(BF16)
            return g_f * state + lax.dot_general(kz, vc, tn_dims, preferred_element_type=F32)

        lax.fori_loop(0, n_chunks, fwd, jnp.zeros((LANES, V_DIM), F32))

        def bwd(t, state):
            c = n_chunks - 1 - t
            r0, qc, kc, vc = load(c)
            cross = jnp.dot(qc, state.astype(BF16), preferred_element_type=F32) * xi_b
            tot = _rms(acc_ref[pl.ds(r0, c_len), vcols] + cross)
            gate = rg_ref[pl.ds(r0, c_len), vcols].astype(F32)
            o_ref[pl.ds(r0, c_len), vcols] = (tot * gate * jax.nn.sigmoid(gate)).astype(BF16)
            kz = (kc.astype(F32) * zeta_b).astype(BF16)
            return g_b * state + lax.dot_general(kz, vc, tn_dims, preferred_element_type=F32)

        lax.fori_loop(0, n_chunks, bwd, jnp.zeros((LANES, V_DIM), F32))


def _ret(dec, rq, rk, rv, rg):
    batch, pairs, seq, _ = rq.shape
    pair_w = 2 * V_DIM
    return pl.pallas_call(
        _ret_kernel,
        grid=(batch, pairs),
        in_specs=[
            pl.BlockSpec((2, 2, RET_CHUNK), lambda b, p: (p, 0, 0)),
            pl.BlockSpec((None, None, seq, LANES), lambda b, p: (b, p, 0, 0)),
            pl.BlockSpec((None, None, seq, LANES), lambda b, p: (b, p, 0, 0)),
            pl.BlockSpec((None, None, seq, pair_w), lambda b, p: (b, p, 0, 0)),
            pl.BlockSpec((None, None, seq, pair_w), lambda b, p: (b, p, 0, 0)),
        ],
        out_specs=pl.BlockSpec((seq, pair_w), lambda b, p: (b, p)),
        out_shape=jax.ShapeDtypeStruct((batch * seq, RET_V_W), BF16),
        scratch_shapes=[pltpu.VMEM((seq, pair_w), F32)],
        compiler_params=pltpu.CompilerParams(
            dimension_semantics=("parallel", "parallel"), vmem_limit_bytes=VMEM_LIMIT),
        name="retention",
    )(dec, rq, rk, rv, rg)


def _merge_kernel(x_ref, a_ref, r_ref, ga_ref, gb_ref, wa_ref, wr_ref, wo_ref, o_ref):
    ya = jnp.dot(a_ref[...], wa_ref[...], preferred_element_type=F32)
    yb = jnp.dot(r_ref[...], wr_ref[...], preferred_element_type=F32)
    m = (jax.nn.sigmoid(ga_ref[...].astype(F32)) * ya
         + jax.nn.sigmoid(gb_ref[...].astype(F32)) * yb)
    o_ref[...] = x_ref[...] + jnp.dot(m.astype(BF16), wo_ref[...], preferred_element_type=F32)


def _merge(x2d, a, r, ga, gb, wa, wr, wo):
    tokens = x2d.shape[0]
    tm = MERGE_TM
    row = lambda w: pl.BlockSpec((tm, w), lambda i: (i, 0))
    return pl.pallas_call(
        _merge_kernel,
        grid=(tokens // tm,),
        in_specs=[row(D_MODEL), row(DIFF_W), row(RET_V_W), row(D_MODEL), row(D_MODEL),
                  _const_spec(wa.shape), _const_spec(wr.shape), _const_spec(wo.shape)],
        out_specs=row(D_MODEL),
        out_shape=jax.ShapeDtypeStruct((tokens, D_MODEL), F32),
        compiler_params=pltpu.CompilerParams(
            dimension_semantics=("parallel",), vmem_limit_bytes=VMEM_LIMIT),
        name="merge",
    )(x2d, a, r, ga, gb, wa, wr, wo)


def _ffn_kernel(x_ref, g2_ref, gf_ref, wg_ref, wu_ref, wd_ref, o_ref):
    x = x_ref[...]
    h = (_rms(x) * g2_ref[...]).astype(BF16)
    gate = jnp.dot(h, wg_ref[...], preferred_element_type=F32)
    up = jnp.dot(h, wu_ref[...], preferred_element_type=F32)
    act = (gate * jax.nn.sigmoid(gate) * up).astype(BF16)
    x = x + jnp.dot(act, wd_ref[...], preferred_element_type=F32)
    o_ref[...] = _rms(x) * gf_ref[...]


def _ffn(x1, g_ffn, g_final, wg, wu, wd):
    tokens = x1.shape[0]
    tm = FFN_TM
    row = pl.BlockSpec((tm, D_MODEL), lambda i: (i, 0))
    return pl.pallas_call(
        _ffn_kernel,
        grid=(tokens // tm,),
        in_specs=[row, _const_spec((1, D_MODEL)), _const_spec((1, D_MODEL)),
                  _const_spec(wg.shape), _const_spec(wu.shape), _const_spec(wd.shape)],
        out_specs=row,
        out_shape=jax.ShapeDtypeStruct((tokens, D_MODEL), F32),
        compiler_params=pltpu.CompilerParams(
            dimension_semantics=("parallel",), vmem_limit_bytes=VMEM_LIMIT),
        name="ffn",
    )(x1, g_ffn, g_final, wg, wu, wd)


def _rope_tables(seq):
    half = QK_DIM // 2
    inv_freq = ROPE_THETA ** (-jnp.arange(0, QK_DIM, 2, dtype=F32) / QK_DIM)
    ang = jnp.arange(seq, dtype=F32)[:, None] * inv_freq[None, :]
    cos, sin = jnp.cos(ang), jnp.sin(ang)
    cos_t = jnp.tile(cos, (1, LANES // half))
    sin_t = jnp.tile(jnp.concatenate([-sin, sin], axis=-1), (1, LANES // QK_DIM))
    return cos_t, sin_t


def kernel(x, g_mix, w_in, diff_lq1, diff_lk1, diff_lq2, diff_lk2, diff_subln_g,
           ret_decay_fwd, ret_decay_bwd, w_up_diff, w_up_ret, w_o,
           g_ffn, w_ffn_gate, w_ffn_up, w_ffn_down, g_final):
    batch, seq, _ = x.shape
    layer = 0
    x2d = x.reshape(batch * seq, D_MODEL)
    cos_t, sin_t = _rope_tables(seq)

    dq, dk, dv, rq, rk, rv, rg, ga, gb = _proj(
        x2d, g_mix[layer][None, :], w_in[layer].astype(BF16), cos_t, sin_t, batch, seq)

    lam_params = jnp.stack([diff_lq1[layer], diff_lk1[layer], diff_lq2[layer], diff_lk2[layer]])
    a = _attn(lam_params.astype(F32), diff_subln_g[layer][None, :].astype(F32), dq, dk, dv)

    dec = jnp.stack([ret_decay_fwd[layer], ret_decay_bwd[layer]], axis=1).astype(F32)
    dec = jnp.broadcast_to(dec[:, :, None], (HEADS, 2, RET_CHUNK))
    r = _ret(dec, rq, rk, rv, rg)

    x1 = _merge(x2d, a, r, ga, gb, w_up_diff[layer].astype(BF16), w_up_ret[layer].astype(BF16),
                w_o[layer].astype(BF16))
    y = _ffn(x1, g_ffn[layer][None, :], g_final[None, :], w_ffn_gate[layer].astype(BF16),
             w_ffn_up[layer].astype(BF16), w_ffn_down[layer].astype(BF16))
    return y.reshape(batch, seq, D_MODEL)
```

```python
import functools
import math

import jax
import jax.numpy as jnp
from jax import lax
from jax.experimental import pallas as pl
from jax.experimental.pallas import tpu as pltpu

F32 = jnp.float32
BF16 = jnp.bfloat16

D_MODEL = 1024
HEADS = 4
QK_DIM = 64
V_DIM = 128
DIFF_W = HEADS * 2 * QK_DIM
RET_QK_W = HEADS * QK_DIM
RET_V_W = HEADS * V_DIM
IN_COLS = 3 * DIFF_W + 2 * RET_QK_W + 2 * RET_V_W + 2 * D_MODEL
D_FF = 2816
ROPE_THETA = 10000.0
NORM_EPS = 1e-5
LAM_INIT = 0.8 - 0.6 * math.exp(-0.3 * 0)
LOG2E = 1.4426950408889634

LANES = 128
VMEM_LIMIT = 56 * 1024 * 1024
PROJ_TM = 512
ATTN_TQ = 256
ATTN_ONES_ROWS = 16
ATTN_KB = 512
ATTN_PB = 256
RET_CHUNK = 256
MERGE_TM = 512
FFN_TM = 512


def _rms(x, eps=NORM_EPS):
    return x * lax.rsqrt(jnp.mean(x * x, axis=-1, keepdims=True) + eps)


def _const_spec(shape):
    nd = len(shape)
    return pl.BlockSpec(shape, lambda *_: (0,) * nd, pipeline_mode=pl.Buffered(1))


def _proj_kernel(x_ref, g_ref, w_ref, cos_ref, sin_ref,
                 dq_ref, dk_ref, dv_ref, rq_ref, rk_ref, rv_ref, rg_ref, ga_ref, gb_ref):
    tm = x_ref.shape[0]
    hb = (_rms(x_ref[...]) * g_ref[...]).astype(BF16)
    cos = cos_ref[...]
    sin = sin_ref[...]
    lane = lax.broadcasted_iota(jnp.int32, (tm, LANES), 1)
    first_half = (lane % QK_DIM) < (QK_DIM // 2)

    def proj(c0, n):
        return jnp.dot(hb, w_ref[:, c0:c0 + n], preferred_element_type=F32)

    def rope(y, scale):
        partner = jnp.where(first_half, pltpu.roll(y, LANES - 32, 1), pltpu.roll(y, 32, 1))
        return ((y * cos + partner * sin) * scale).astype(BF16)

    q_scale = QK_DIM ** -0.5 * LOG2E
    y = proj(0, DIFF_W)
    for h in range(HEADS):
        dq_ref[h] = rope(y[:, h * LANES:(h + 1) * LANES], q_scale)
    y = proj(DIFF_W, DIFF_W)
    for h in range(HEADS):
        dk_ref[h] = rope(y[:, h * LANES:(h + 1) * LANES], 1.0)
    y = proj(2 * DIFF_W, DIFF_W)
    for h in range(HEADS):
        dv_ref[h] = y[:, h * LANES:(h + 1) * LANES].astype(BF16)
    c0 = 3 * DIFF_W
    y = proj(c0, 2 * RET_QK_W)
    for p in range(HEADS // 2):
        rq_ref[p] = rope(y[:, p * LANES:(p + 1) * LANES], 1.0)
        rk_ref[p] = rope(y[:, RET_QK_W + p * LANES:RET_QK_W + (p + 1) * LANES], QK_DIM ** -0.5)
    c0 += 2 * RET_QK_W
    y = proj(c0, RET_V_W)
    for p in range(HEADS // 2):
        rv_ref[p] = y[:, p * 2 * V_DIM:(p + 1) * 2 * V_DIM].astype(BF16)
    c0 += RET_V_W
    y = proj(c0, RET_V_W)
    for p in range(HEADS // 2):
        rg_ref[p] = y[:, p * 2 * V_DIM:(p + 1) * 2 * V_DIM].astype(BF16)
    c0 += RET_V_W
    for half in range(2):
        ga_ref[:, half * 512:(half + 1) * 512] = proj(c0 + half * 512, 512).astype(BF16)
    c0 += D_MODEL
    for half in range(2):
        gb_ref[:, half * 512:(half + 1) * 512] = proj(c0 + half * 512, 512).astype(BF16)


def _proj(x2d, g_mix, w_in, cos_t, sin_t, batch, seq):
    tm = PROJ_TM
    nb = seq // tm
    tokens = batch * seq

    def head_major(n_groups, width):
        shape = jax.ShapeDtypeStruct((batch, n_groups, seq, width), BF16)
        spec = pl.BlockSpec((None, n_groups, tm, width), lambda i: (i // nb, 0, i % nb, 0))
        return shape, spec

    outs = [head_major(HEADS, LANES)] * 3 + [head_major(HEADS // 2, LANES)] * 2 \
        + [head_major(HEADS // 2, 2 * V_DIM)] * 2
    out_shape = [o[0] for o in outs] + [jax.ShapeDtypeStruct((tokens, D_MODEL), BF16)] * 2
    out_specs = [o[1] for o in outs] + [pl.BlockSpec((tm, D_MODEL), lambda i: (i, 0))] * 2
    return pl.pallas_call(
        _proj_kernel,
        grid=(tokens // tm,),
        in_specs=[
            pl.BlockSpec((tm, D_MODEL), lambda i: (i, 0)),
            _const_spec((1, D_MODEL)),
            _const_spec((D_MODEL, IN_COLS)),
            pl.BlockSpec((tm, LANES), lambda i: (i % nb, 0)),
            pl.BlockSpec((tm, LANES), lambda i: (i % nb, 0)),
        ],
        out_specs=out_specs,
        out_shape=out_shape,
        compiler_params=pltpu.CompilerParams(
            dimension_semantics=("parallel",), vmem_limit_bytes=VMEM_LIMIT),
        name="in_proj",
    )(x2d, g_mix, w_in, cos_t, sin_t)


def _attn_kernel(lam_ref, g_ref, q_ref, k_ref, v_ref, o_ref,
                 vext_ref, s0_ref, s1_ref, m0_ref, m1_ref, a0_ref, a1_ref):
    seq = q_ref.shape[0]
    tq = ATTN_TQ
    nq = seq // tq
    nt_dims = (((1,), (1,)), ((), ()))

    vext_ref[:V_DIM, :] = v_ref[...].astype(F32).T.astype(BF16)
    vext_ref[V_DIM:, :] = jnp.ones((ATTN_ONES_ROWS, seq), BF16)

    lp = lam_ref[...]
    lam = (jnp.exp(jnp.sum(lp[0:1] * lp[1:2], axis=-1, keepdims=True))
           - jnp.exp(jnp.sum(lp[2:3] * lp[3:4], axis=-1, keepdims=True)) + LAM_INIT)
    lane = lax.broadcasted_iota(jnp.int32, (tq, LANES), 1)

    bufs = ((s0_ref, m0_ref, a0_ref), (s1_ref, m1_ref, a1_ref))

    def step(qk=None, pv=None, fin=None):
        if fin is not None:
            j, (_, _, fa_ref) = fin
            acc = fa_ref[...]
            o = acc[:V_DIM] / acc[V_DIM:V_DIM + 1]
            o = (o[:, :tq] - lam * o[:, tq:]).T
            o = _rms(o) * g_ref[...] * (1.0 - LAM_INIT)
            o_ref[pl.ds(pl.multiple_of(j * tq, tq), tq), :] = o.astype(BF16)
        if qk is not None:
            i, (qs_ref, qm_ref, _) = qk
            q = q_ref[pl.ds(pl.multiple_of(i * tq, tq), tq), :]
            zero = jnp.zeros_like(q)
            q2 = jnp.concatenate([jnp.where(lane < QK_DIM, q, zero),
                                  jnp.where(lane >= QK_DIM, q, zero)], axis=0)
            m = None
        if pv is not None:
            _, (ps_ref, pm_ref, pa_ref) = pv
            m_prev = pm_ref[...]
            acc = None
        for kb in range(seq // ATTN_KB):
            rows = slice(kb * ATTN_KB, (kb + 1) * ATTN_KB)
            if qk is not None:
                s = lax.dot_general(k_ref[rows, :], q2, nt_dims, preferred_element_type=F32)
                qs_ref[rows, :] = s
                blk = jnp.max(s, axis=0, keepdims=True)
                m = blk if m is None else jnp.maximum(m, blk)
            if pv is not None:
                for sub in range(ATTN_KB // ATTN_PB):
                    r0 = kb * ATTN_KB + sub * ATTN_PB
                    rows2 = slice(r0, r0 + ATTN_PB)
                    p = jnp.exp2(ps_ref[rows2, :] - m_prev).astype(BF16)
                    part = jnp.dot(vext_ref[:, rows2], p, preferred_element_type=F32)
                    acc = part if acc is None else acc + part
        if qk is not None:
            qm_ref[...] = m
        if pv is not None:
            pa_ref[...] = acc

    def unit(t, parity):
        return t, bufs[parity]

    step(qk=unit(0, 0))
    step(qk=unit(1, 1), pv=unit(0, 0))

    def body(jj, carry):
        t = 2 * jj
        @pl.when(jj > 0)
        def _():
            step(qk=unit(t, 0), pv=unit(t - 1, 1), fin=unit(t - 2, 0))

        @pl.when(jj < nq)
        def _():
            step(qk=unit(t + 1, 1), pv=unit(t, 0), fin=unit(t - 1, 1))
        return carry

    lax.fori_loop(1, nq // 2, body, 0)

    step(pv=unit(nq - 1, 1), fin=unit(nq - 2, 0))
    step(fin=unit(nq - 1, 1))


def _attn(lam_params, g_sub, dq, dk, dv):
    batch, heads, seq, _ = dq.shape
    tq = ATTN_TQ
    assert (seq // tq) % 2 == 0
    seq_spec = pl.BlockSpec((None, None, seq, LANES), lambda b, h: (b, h, 0, 0))
    return pl.pallas_call(
        _attn_kernel,
        grid=(batch, heads),
        in_specs=[_const_spec((4, QK_DIM)), _const_spec((1, V_DIM)), seq_spec, seq_spec, seq_spec],
        out_specs=pl.BlockSpec((seq, V_DIM), lambda b, h: (b, h)),
        out_shape=jax.ShapeDtypeStruct((batch * seq, HEADS * V_DIM), BF16),
        scratch_shapes=[
            pltpu.VMEM((V_DIM + ATTN_ONES_ROWS, seq), BF16),
            pltpu.VMEM((seq, 2 * tq), F32), pltpu.VMEM((seq, 2 * tq), F32),
            pltpu.VMEM((1, 2 * tq), F32), pltpu.VMEM((1, 2 * tq), F32),
            pltpu.VMEM((V_DIM + ATTN_ONES_ROWS, 2 * tq), F32),
            pltpu.VMEM((V_DIM + ATTN_ONES_ROWS, 2 * tq), F32),
        ],
        compiler_params=pltpu.CompilerParams(
            dimension_semantics=("parallel", "parallel"), vmem_limit_bytes=VMEM_LIMIT),
        name="diff_attn",
    )(lam_params, g_sub, dq, dk, dv)


def _log_sigmoid(x):
    return jnp.minimum(x, 0.0) - jnp.log1p(jnp.exp(-jnp.abs(x)))


def _ret_kernel(dec_ref, q_ref, k_ref, v_ref, rg_ref, o_ref, dmat_ref, u_ref, sf_ref, sb_ref):
    seq = q_ref.shape[0]
    c_len = RET_CHUNK
    n_chunks = seq // c_len
    pair_w = 2 * V_DIM
    nt_dims = (((1,), (1,)), ((), ()))
    tn_dims = (((0,), (0,)), ((), ()))

    lg = _log_sigmoid(dec_ref[...])
    row_i = lax.broadcasted_iota(jnp.int32, (c_len, c_len), 0)
    col_i = lax.broadcasted_iota(jnp.int32, (c_len, c_len), 1)
    diff = (row_i - col_i).astype(F32)
    for hh in range(2):
        dmat_ref[hh * c_len:(hh + 1) * c_len, :] = jnp.exp(
            jnp.where(diff >= 0, diff * lg[4 + 2 * hh:5 + 2 * hh], -diff * lg[5 + 2 * hh:6 + 2 * hh]))
    rows_k = lax.broadcasted_iota(jnp.int32, (c_len, LANES), 0).astype(F32)
    rows_v = lax.broadcasted_iota(jnp.int32, (c_len, pair_w), 0).astype(F32)
    zeta_f = jnp.exp((c_len - 1 - rows_k) * lg[0:1, :LANES])
    zeta_b = jnp.exp(rows_k * lg[1:2, :LANES])
    xi_f = jnp.exp((rows_v + 1) * lg[2:3])
    xi_b = jnp.exp((c_len - rows_v) * lg[3:4])
    srow = lax.broadcasted_iota(jnp.int32, (LANES, pair_w), 0)
    scol = lax.broadcasted_iota(jnp.int32, (LANES, pair_w), 1)
    head_a_row = srow < QK_DIM
    same_head = head_a_row == (scol < V_DIM)
    g_f = jnp.exp(c_len * jnp.where(head_a_row, lg[4:5], lg[6:7]))
    g_b = jnp.exp(c_len * jnp.where(head_a_row, lg[5:6], lg[7:8]))
    lane = lax.broadcasted_iota(jnp.int32, (c_len, LANES), 1)

    def chunk(c):
        return pl.ds(pl.multiple_of(c * c_len, c_len), c_len)

    def phase_a(c, carry):
        kc = k_ref[chunk(c), :].astype(F32)
        kz = jnp.concatenate([kc * zeta_f, kc * zeta_b], axis=1).astype(BF16)
        u_ref[c] = lax.dot_general(kz, v_ref[chunk(c), :], tn_dims, preferred_element_type=F32)
        return carry

    lax.fori_loop(0, n_chunks, phase_a, 0, unroll=2)

    def phase_b(t, carry):
        s_f, s_b = carry
        sf_ref[t] = jnp.where(same_head, s_f, 0.0).astype(BF16)
        s_f = g_f * s_f + u_ref[t, :LANES, :]
        cb = n_chunks - 1 - t
        sb_ref[cb] = jnp.where(same_head, s_b, 0.0).astype(BF16)
        s_b = g_b * s_b + u_ref[cb, LANES:, :]
        return s_f, s_b

    zero_state = jnp.zeros((LANES, pair_w), F32)
    lax.fori_loop(0, n_chunks, phase_b, (zero_state, zero_state))

    def phase_c(c, carry):
        q = q_ref[chunk(c), :]
        zero = jnp.zeros_like(q)
        q2 = jnp.concatenate([jnp.where(lane < QK_DIM, q, zero),
                              jnp.where(lane >= QK_DIM, q, zero)], axis=0)
        vc = v_ref[chunk(c), :]
        sc = lax.dot_general(q2, k_ref[chunk(c), :], nt_dims, preferred_element_type=F32) * dmat_ref[...]
        inner = jnp.dot(sc.astype(BF16), vc, preferred_element_type=F32)
        tot = (jnp.concatenate([inner[:c_len, :V_DIM], inner[c_len:, V_DIM:]], axis=1)
               + jnp.dot(q, sf_ref[c], preferred_element_type=F32) * xi_f
               + jnp.dot(q, sb_ref[c], preferred_element_type=F32) * xi_b)
        r = jnp.concatenate([_rms(tot[:, :V_DIM]), _rms(tot[:, V_DIM:])], axis=1)
        gate = rg_ref[chunk(c), :].astype(F32)
        o_ref[chunk(c), :] = (r * gate * jax.nn.sigmoid(gate)).astype(BF16)
        return carry

    lax.fori_loop(0, n_chunks, phase_c, 0, unroll=2)


def _ret(dec, rq, rk, rv, rg):
    batch, pairs, seq, _ = rq.shape
    pair_w = 2 * V_DIM
    n_chunks = seq // RET_CHUNK
    assert RET_CHUNK == pair_w
    return pl.pallas_call(
        _ret_kernel,
        grid=(batch, pairs),
        in_specs=[
            pl.BlockSpec((None, 8, pair_w), lambda b, p: (p, 0, 0)),
            pl.BlockSpec((None, None, seq, LANES), lambda b, p: (b, p, 0, 0)),
            pl.BlockSpec((None, None, seq, LANES), lambda b, p: (b, p, 0, 0)),
            pl.BlockSpec((None, None, seq, pair_w), lambda b, p: (b, p, 0, 0)),
            pl.BlockSpec((None, None, seq, pair_w), lambda b, p: (b, p, 0, 0)),
        ],
        out_specs=pl.BlockSpec((seq, pair_w), lambda b, p: (b, p)),
        out_shape=jax.ShapeDtypeStruct((batch * seq, RET_V_W), BF16),
        scratch_shapes=[
            pltpu.VMEM((2 * RET_CHUNK, RET_CHUNK), F32),
            pltpu.VMEM((n_chunks, 2 * LANES, pair_w), F32),
            pltpu.VMEM((n_chunks, LANES, pair_w), BF16),
            pltpu.VMEM((n_chunks, LANES, pair_w), BF16),
        ],
        compiler_params=pltpu.CompilerParams(
            dimension_semantics=("parallel", "parallel"), vmem_limit_bytes=VMEM_LIMIT),
        name="retention",
    )(dec, rq, rk, rv, rg)


def _merge_kernel(x_ref, a_ref, r_ref, ga_ref, gb_ref, wa_ref, wr_ref, wo_ref, o_ref):
    ya = jnp.dot(a_ref[...], wa_ref[...], preferred_element_type=F32)
    yb = jnp.dot(r_ref[...], wr_ref[...], preferred_element_type=F32)
    m = (jax.nn.sigmoid(ga_ref[...].astype(F32)) * ya
         + jax.nn.sigmoid(gb_ref[...].astype(F32)) * yb)
    o_ref[...] = x_ref[...] + jnp.dot(m.astype(BF16), wo_ref[...], preferred_element_type=F32)


def _merge(x2d, a, r, ga, gb, wa, wr, wo):
    tokens = x2d.shape[0]
    tm = MERGE_TM
    row = lambda w: pl.BlockSpec((tm, w), lambda i: (i, 0))
    return pl.pallas_call(
        _merge_kernel,
        grid=(tokens // tm,),
        in_specs=[row(D_MODEL), row(DIFF_W), row(RET_V_W), row(D_MODEL), row(D_MODEL),
                  _const_spec(wa.shape), _const_spec(wr.shape), _const_spec(wo.shape)],
        out_specs=row(D_MODEL),
        out_shape=jax.ShapeDtypeStruct((tokens, D_MODEL), F32),
        compiler_params=pltpu.CompilerParams(
            dimension_semantics=("parallel",), vmem_limit_bytes=VMEM_LIMIT),
        name="merge",
    )(x2d, a, r, ga, gb, wa, wr, wo)


def _ffn_kernel(x_ref, g2_ref, gf_ref, wg_ref, wu_ref, wd_ref, o_ref):
    x = x_ref[...]
    h = (_rms(x) * g2_ref[...]).astype(BF16)
    gate = jnp.dot(h, wg_ref[...], preferred_element_type=F32)
    up = jnp.dot(h, wu_ref[...], preferred_element_type=F32)
    act = (gate * jax.nn.sigmoid(gate) * up).astype(BF16)
    x = x + jnp.dot(act, wd_ref[...], preferred_element_type=F32)
    o_ref[...] = _rms(x) * gf_ref[...]


def _ffn(x1, g_ffn, g_final, wg, wu, wd):
    tokens = x1.shape[0]
    tm = FFN_TM
    row = pl.BlockSpec((tm, D_MODEL), lambda i: (i, 0))
    return pl.pallas_call(
        _ffn_kernel,
        grid=(tokens // tm,),
        in_specs=[row, _const_spec((1, D_MODEL)), _const_spec((1, D_MODEL)),
                  _const_spec(wg.shape), _const_spec(wu.shape), _const_spec(wd.shape)],
        out_specs=row,
        out_shape=jax.ShapeDtypeStruct((tokens, D_MODEL), F32),
        compiler_params=pltpu.CompilerParams(
            dimension_semantics=("parallel",), vmem_limit_bytes=VMEM_LIMIT),
        name="ffn",
    )(x1, g_ffn, g_final, wg, wu, wd)


def _rope_tables(seq):
    half = QK_DIM // 2
    inv_freq = ROPE_THETA ** (-jnp.arange(0, QK_DIM, 2, dtype=F32) / QK_DIM)
    ang = jnp.arange(seq, dtype=F32)[:, None] * inv_freq[None, :]
    cos, sin = jnp.cos(ang), jnp.sin(ang)
    cos_t = jnp.tile(cos, (1, LANES // half))
    sin_t = jnp.tile(jnp.concatenate([-sin, sin], axis=-1), (1, LANES // QK_DIM))
    return cos_t, sin_t


def _decay_table(fwd, bwd):
    pairs = HEADS // 2
    f = fwd.astype(F32).reshape(pairs, 2)
    b = bwd.astype(F32).reshape(pairs, 2)
    k_lanes = lambda t: jnp.tile(jnp.repeat(t, QK_DIM, axis=1), (1, 2))
    v_lanes = lambda t: jnp.repeat(t, V_DIM, axis=1)
    all_lanes = lambda col: jnp.broadcast_to(col[:, None], (pairs, 2 * V_DIM))
    rows = [k_lanes(f), k_lanes(b), v_lanes(f), v_lanes(b),
            all_lanes(f[:, 0]), all_lanes(b[:, 0]), all_lanes(f[:, 1]), all_lanes(b[:, 1])]
    return jnp.stack(rows, axis=1)


def kernel(x, g_mix, w_in, diff_lq1, diff_lk1, diff_lq2, diff_lk2, diff_subln_g,
           ret_decay_fwd, ret_decay_bwd, w_up_diff, w_up_ret, w_o,
           g_ffn, w_ffn_gate, w_ffn_up, w_ffn_down, g_final):
    batch, seq, _ = x.shape
    layer = 0
    x2d = x.reshape(batch * seq, D_MODEL)
    cos_t, sin_t = _rope_tables(seq)

    dq, dk, dv, rq, rk, rv, rg, ga, gb = _proj(
        x2d, g_mix[layer][None, :], w_in[layer].astype(BF16), cos_t, sin_t, batch, seq)

    lam_params = jnp.stack([diff_lq1[layer], diff_lk1[layer], diff_lq2[layer], diff_lk2[layer]])
    a = _attn(lam_params.astype(F32), diff_subln_g[layer][None, :].astype(F32), dq, dk, dv)

    r = _ret(_decay_table(ret_decay_fwd[layer], ret_decay_bwd[layer]), rq, rk, rv, rg)

    x1 = _merge(x2d, a, r, ga, gb, w_up_diff[layer].astype(BF16), w_up_ret[layer].astype(BF16),
                w_o[layer].astype(BF16))
    y = _ffn(x1, g_ffn[layer][None, :], g_final[None, :], w_ffn_gate[layer].astype(BF16),
             w_ffn_up[layer].astype(BF16), w_ffn_down[layer].astype(BF16))
    return y.reshape(batch, seq, D_MODEL)
```

```python
import functools
import math

import jax
import jax.numpy as jnp
from jax import lax
from jax.experimental import pallas as pl
from jax.experimental.pallas import tpu as pltpu

F32 = jnp.float32
BF16 = jnp.bfloat16

D_MODEL = 1024
HEADS = 4
QK_DIM = 64
V_DIM = 128
DIFF_W = HEADS * 2 * QK_DIM
RET_QK_W = HEADS * QK_DIM
RET_V_W = HEADS * V_DIM
IN_COLS = 3 * DIFF_W + 2 * RET_QK_W + 2 * RET_V_W + 2 * D_MODEL
D_FF = 2816
ROPE_THETA = 10000.0
NORM_EPS = 1e-5
LAM_INIT = 0.8 - 0.6 * math.exp(-0.3 * 0)
LOG2E = 1.4426950408889634

LANES = 128
VMEM_LIMIT = 56 * 1024 * 1024
PROJ_TM = 512
ATTN_TQ = 256
ATTN_ONES_ROWS = 16
ATTN_KB = 512
ATTN_PB = 256
RET_CHUNK = 256
POST_TM = 512


def _rms(x, eps=NORM_EPS):
    return x * lax.rsqrt(jnp.mean(x * x, axis=-1, keepdims=True) + eps)


def _const_spec(shape):
    nd = len(shape)
    return pl.BlockSpec(shape, lambda *_: (0,) * nd, pipeline_mode=pl.Buffered(1))


def _proj_kernel(x_ref, g_ref, w_ref, cos_ref, sin_ref,
                 dq_ref, dk_ref, dv_ref, rq_ref, rk_ref, rv_ref, rg_ref, ga_ref, gb_ref):
    tm = x_ref.shape[0]
    hb = (_rms(x_ref[...]) * g_ref[...]).astype(BF16)
    cos = cos_ref[...]
    sin = sin_ref[...]
    lane = lax.broadcasted_iota(jnp.int32, (tm, LANES), 1)
    first_half = (lane % QK_DIM) < (QK_DIM // 2)

    def proj(c0, n):
        return jnp.dot(hb, w_ref[:, c0:c0 + n], preferred_element_type=F32)

    def rope(y, scale):
        partner = jnp.where(first_half, pltpu.roll(y, LANES - 32, 1), pltpu.roll(y, 32, 1))
        return ((y * cos + partner * sin) * scale).astype(BF16)

    q_scale = QK_DIM ** -0.5 * LOG2E
    y = proj(0, DIFF_W)
    for h in range(HEADS):
        dq_ref[h] = rope(y[:, h * LANES:(h + 1) * LANES], q_scale)
    y = proj(DIFF_W, DIFF_W)
    for h in range(HEADS):
        dk_ref[h] = rope(y[:, h * LANES:(h + 1) * LANES], 1.0)
    y = proj(2 * DIFF_W, DIFF_W)
    for h in range(HEADS):
        dv_ref[h] = y[:, h * LANES:(h + 1) * LANES].astype(BF16)
    c0 = 3 * DIFF_W
    y = proj(c0, 2 * RET_QK_W)
    for p in range(HEADS // 2):
        rq_ref[p] = rope(y[:, p * LANES:(p + 1) * LANES], 1.0)
        rk_ref[p] = rope(y[:, RET_QK_W + p * LANES:RET_QK_W + (p + 1) * LANES], QK_DIM ** -0.5)
    c0 += 2 * RET_QK_W
    y = proj(c0, RET_V_W)
    for p in range(HEADS // 2):
        rv_ref[p] = y[:, p * 2 * V_DIM:(p + 1) * 2 * V_DIM].astype(BF16)
    c0 += RET_V_W
    y = proj(c0, RET_V_W)
    for p in range(HEADS // 2):
        rg_ref[p] = y[:, p * 2 * V_DIM:(p + 1) * 2 * V_DIM].astype(BF16)
    c0 += RET_V_W
    for half in range(2):
        ga_ref[:, half * 512:(half + 1) * 512] = proj(c0 + half * 512, 512).astype(BF16)
    c0 += D_MODEL
    for half in range(2):
        gb_ref[:, half * 512:(half + 1) * 512] = proj(c0 + half * 512, 512).astype(BF16)


def _proj(x2d, g_mix, w_in, cos_t, sin_t, batch, seq):
    tm = PROJ_TM
    nb = seq // tm
    tokens = batch * seq

    def head_major(n_groups, width):
        shape = jax.ShapeDtypeStruct((batch, n_groups, seq, width), BF16)
        spec = pl.BlockSpec((None, n_groups, tm, width), lambda i: (i // nb, 0, i % nb, 0))
        return shape, spec

    outs = [head_major(HEADS, LANES)] * 3 + [head_major(HEADS // 2, LANES)] * 2 \
        + [head_major(HEADS // 2, 2 * V_DIM)] * 2
    out_shape = [o[0] for o in outs] + [jax.ShapeDtypeStruct((tokens, D_MODEL), BF16)] * 2
    out_specs = [o[1] for o in outs] + [pl.BlockSpec((tm, D_MODEL), lambda i: (i, 0))] * 2
    return pl.pallas_call(
        _proj_kernel,
        grid=(tokens // tm,),
        in_specs=[
            pl.BlockSpec((tm, D_MODEL), lambda i: (i, 0)),
            _const_spec((1, D_MODEL)),
            _const_spec((D_MODEL, IN_COLS)),
            pl.BlockSpec((tm, LANES), lambda i: (i % nb, 0)),
            pl.BlockSpec((tm, LANES), lambda i: (i % nb, 0)),
        ],
        out_specs=out_specs,
        out_shape=out_shape,
        compiler_params=pltpu.CompilerParams(
            dimension_semantics=("parallel",), vmem_limit_bytes=VMEM_LIMIT),
        name="in_proj",
    )(x2d, g_mix, w_in, cos_t, sin_t)


def _attn_kernel(lam_ref, g_ref, q_ref, k_ref, v_ref, o_ref,
                 vext_ref, s0_ref, s1_ref, m0_ref, m1_ref, a0_ref, a1_ref):
    seq = q_ref.shape[0]
    tq = ATTN_TQ
    nq = seq // tq
    nt_dims = (((1,), (1,)), ((), ()))

    vext_ref[:V_DIM, :] = v_ref[...].astype(F32).T.astype(BF16)
    vext_ref[V_DIM:, :] = jnp.ones((ATTN_ONES_ROWS, seq), BF16)

    lp = lam_ref[...]
    lam = (jnp.exp(jnp.sum(lp[0:1] * lp[1:2], axis=-1, keepdims=True))
           - jnp.exp(jnp.sum(lp[2:3] * lp[3:4], axis=-1, keepdims=True)) + LAM_INIT)
    lane = lax.broadcasted_iota(jnp.int32, (tq, LANES), 1)

    bufs = ((s0_ref, m0_ref, a0_ref), (s1_ref, m1_ref, a1_ref))

    def step(qk=None, pv=None, fin=None):
        if fin is not None:
            j, (_, _, fa_ref) = fin
            acc = fa_ref[...]
            o = acc[:V_DIM] / acc[V_DIM:V_DIM + 1]
            o = (o[:, :tq] - lam * o[:, tq:]).T
            o = _rms(o) * g_ref[...] * (1.0 - LAM_INIT)
            o_ref[pl.ds(pl.multiple_of(j * tq, tq), tq), :] = o.astype(BF16)
        if qk is not None:
            i, (qs_ref, qm_ref, _) = qk
            q = q_ref[pl.ds(pl.multiple_of(i * tq, tq), tq), :]
            zero = jnp.zeros_like(q)
            q2 = jnp.concatenate([jnp.where(lane < QK_DIM, q, zero),
                                  jnp.where(lane >= QK_DIM, q, zero)], axis=0)
            m = None
        if pv is not None:
            _, (ps_ref, pm_ref, pa_ref) = pv
            m_prev = pm_ref[...]
            acc = None
        for kb in range(seq // ATTN_KB):
            rows = slice(kb * ATTN_KB, (kb + 1) * ATTN_KB)
            if qk is not None:
                s = lax.dot_general(k_ref[rows, :], q2, nt_dims, preferred_element_type=F32)
                qs_ref[rows, :] = s
                blk = jnp.max(s, axis=0, keepdims=True)
                m = blk if m is None else jnp.maximum(m, blk)
            if pv is not None:
                for sub in range(ATTN_KB // ATTN_PB):
                    r0 = kb * ATTN_KB + sub * ATTN_PB
                    rows2 = slice(r0, r0 + ATTN_PB)
                    p = jnp.exp2(ps_ref[rows2, :] - m_prev).astype(BF16)
                    part = jnp.dot(vext_ref[:, rows2], p, preferred_element_type=F32)
                    acc = part if acc is None else acc + part
        if qk is not None:
            qm_ref[...] = m
        if pv is not None:
            pa_ref[...] = acc

    def unit(t, parity):
        return t, bufs[parity]

    step(qk=unit(0, 0))
    step(qk=unit(1, 1), pv=unit(0, 0))

    def body(jj, carry):
        t = 2 * jj
        @pl.when(jj > 0)
        def _():
            step(qk=unit(t, 0), pv=unit(t - 1, 1), fin=unit(t - 2, 0))

        @pl.when(jj < nq)
        def _():
            step(qk=unit(t + 1, 1), pv=unit(t, 0), fin=unit(t - 1, 1))
        return carry

    lax.fori_loop(1, nq // 2, body, 0)

    step(pv=unit(nq - 1, 1), fin=unit(nq - 2, 0))
    step(fin=unit(nq - 1, 1))


def _attn(lam_params, g_sub, dq, dk, dv):
    batch, heads, seq, _ = dq.shape
    tq = ATTN_TQ
    assert (seq // tq) % 2 == 0
    seq_spec = pl.BlockSpec((None, None, seq, LANES), lambda b, h: (b, h, 0, 0))
    return pl.pallas_call(
        _attn_kernel,
        grid=(batch, heads),
        in_specs=[_const_spec((4, QK_DIM)), _const_spec((1, V_DIM)), seq_spec, seq_spec, seq_spec],
        out_specs=pl.BlockSpec((seq, V_DIM), lambda b, h: (b, h)),
        out_shape=jax.ShapeDtypeStruct((batch * seq, HEADS * V_DIM), BF16),
        scratch_shapes=[
            pltpu.VMEM((V_DIM + ATTN_ONES_ROWS, seq), BF16),
            pltpu.VMEM((seq, 2 * tq), F32), pltpu.VMEM((seq, 2 * tq), F32),
            pltpu.VMEM((1, 2 * tq), F32), pltpu.VMEM((1, 2 * tq), F32),
            pltpu.VMEM((V_DIM + ATTN_ONES_ROWS, 2 * tq), F32),
            pltpu.VMEM((V_DIM + ATTN_ONES_ROWS, 2 * tq), F32),
        ],
        compiler_params=pltpu.CompilerParams(
            dimension_semantics=("parallel", "parallel"), vmem_limit_bytes=VMEM_LIMIT),
        name="diff_attn",
    )(lam_params, g_sub, dq, dk, dv)


def _log_sigmoid(x):
    return jnp.minimum(x, 0.0) - jnp.log1p(jnp.exp(-jnp.abs(x)))


def _ret_kernel(dec_ref, q_ref, k_ref, v_ref, rg_ref, o_ref, dmat_ref, u_ref, sf_ref, sb_ref):
    seq = q_ref.shape[0]
    c_len = RET_CHUNK
    n_chunks = seq // c_len
    pair_w = 2 * V_DIM
    nt_dims = (((1,), (1,)), ((), ()))
    tn_dims = (((0,), (0,)), ((), ()))

    lg = _log_sigmoid(dec_ref[...])
    row_i = lax.broadcasted_iota(jnp.int32, (c_len, c_len), 0)
    col_i = lax.broadcasted_iota(jnp.int32, (c_len, c_len), 1)
    diff = (row_i - col_i).astype(F32)
    for hh in range(2):
        dmat_ref[hh * c_len:(hh + 1) * c_len, :] = jnp.exp(
            jnp.where(diff >= 0, diff * lg[4 + 2 * hh:5 + 2 * hh], -diff * lg[5 + 2 * hh:6 + 2 * hh]))
    rows_k = lax.broadcasted_iota(jnp.int32, (c_len, LANES), 0).astype(F32)
    rows_v = lax.broadcasted_iota(jnp.int32, (c_len, pair_w), 0).astype(F32)
    zeta_f = jnp.exp((c_len - 1 - rows_k) * lg[0:1, :LANES])
    zeta_b = jnp.exp(rows_k * lg[1:2, :LANES])
    xi_f = jnp.exp((rows_v + 1) * lg[2:3])
    xi_b = jnp.exp((c_len - rows_v) * lg[3:4])
    srow = lax.broadcasted_iota(jnp.int32, (LANES, pair_w), 0)
    scol = lax.broadcasted_iota(jnp.int32, (LANES, pair_w), 1)
    head_a_row = srow < QK_DIM
    same_head = head_a_row == (scol < V_DIM)
    g_f = jnp.exp(c_len * jnp.where(head_a_row, lg[4:5], lg[6:7]))
    g_b = jnp.exp(c_len * jnp.where(head_a_row, lg[5:6], lg[7:8]))
    lane = lax.broadcasted_iota(jnp.int32, (c_len, LANES), 1)

    def chunk(c):
        return pl.ds(pl.multiple_of(c * c_len, c_len), c_len)

    def phase_a(c, carry):
        kc = k_ref[chunk(c), :].astype(F32)
        kz = jnp.concatenate([kc * zeta_f, kc * zeta_b], axis=1).astype(BF16)
        u_ref[c] = lax.dot_general(kz, v_ref[chunk(c), :], tn_dims, preferred_element_type=F32)
        return carry

    lax.fori_loop(0, n_chunks, phase_a, 0, unroll=2)

    def phase_b(t, carry):
        s_f, s_b = carry
        sf_ref[t] = jnp.where(same_head, s_f, 0.0).astype(BF16)
        s_f = g_f * s_f + u_ref[t, :LANES, :]
        cb = n_chunks - 1 - t
        sb_ref[cb] = jnp.where(same_head, s_b, 0.0).astype(BF16)
        s_b = g_b * s_b + u_ref[cb, LANES:, :]
        return s_f, s_b

    zero_state = jnp.zeros((LANES, pair_w), F32)
    lax.fori_loop(0, n_chunks, phase_b, (zero_state, zero_state))

    def phase_c(c, carry):
        q = q_ref[chunk(c), :]
        zero = jnp.zeros_like(q)
        q2 = jnp.concatenate([jnp.where(lane < QK_DIM, q, zero),
                              jnp.where(lane >= QK_DIM, q, zero)], axis=0)
        vc = v_ref[chunk(c), :]
        sc = lax.dot_general(q2, k_ref[chunk(c), :], nt_dims, preferred_element_type=F32) * dmat_ref[...]
        inner = jnp.dot(sc.astype(BF16), vc, preferred_element_type=F32)
        tot = (jnp.concatenate([inner[:c_len, :V_DIM], inner[c_len:, V_DIM:]], axis=1)
               + jnp.dot(q, sf_ref[c], preferred_element_type=F32) * xi_f
               + jnp.dot(q, sb_ref[c], preferred_element_type=F32) * xi_b)
        r = jnp.concatenate([_rms(tot[:, :V_DIM]), _rms(tot[:, V_DIM:])], axis=1)
        gate = rg_ref[chunk(c), :].astype(F32)
        o_ref[chunk(c), :] = (r * gate * jax.nn.sigmoid(gate)).astype(BF16)
        return carry

    lax.fori_loop(0, n_chunks, phase_c, 0, unroll=2)


def _ret(dec, rq, rk, rv, rg):
    batch, pairs, seq, _ = rq.shape
    pair_w = 2 * V_DIM
    n_chunks = seq // RET_CHUNK
    assert RET_CHUNK == pair_w
    return pl.pallas_call(
        _ret_kernel,
        grid=(batch, pairs),
        in_specs=[
            pl.BlockSpec((None, 8, pair_w), lambda b, p: (p, 0, 0)),
            pl.BlockSpec((None, None, seq, LANES), lambda b, p: (b, p, 0, 0)),
            pl.BlockSpec((None, None, seq, LANES), lambda b, p: (b, p, 0, 0)),
            pl.BlockSpec((None, None, seq, pair_w), lambda b, p: (b, p, 0, 0)),
            pl.BlockSpec((None, None, seq, pair_w), lambda b, p: (b, p, 0, 0)),
        ],
        out_specs=pl.BlockSpec((seq, pair_w), lambda b, p: (b, p)),
        out_shape=jax.ShapeDtypeStruct((batch * seq, RET_V_W), BF16),
        scratch_shapes=[
            pltpu.VMEM((2 * RET_CHUNK, RET_CHUNK), F32),
            pltpu.VMEM((n_chunks, 2 * LANES, pair_w), F32),
            pltpu.VMEM((n_chunks, LANES, pair_w), BF16),
            pltpu.VMEM((n_chunks, LANES, pair_w), BF16),
        ],
        compiler_params=pltpu.CompilerParams(
            dimension_semantics=("parallel", "parallel"), vmem_limit_bytes=VMEM_LIMIT),
        name="retention",
    )(dec, rq, rk, rv, rg)


def _post_kernel(x_ref, a_ref, r_ref, ga_ref, gb_ref, wa_ref, wr_ref, wo_ref,
                 g2_ref, gf_ref, wg_ref, wu_ref, wd_ref, o_ref):
    ya = jnp.dot(a_ref[...], wa_ref[...], preferred_element_type=F32)
    yb = jnp.dot(r_ref[...], wr_ref[...], preferred_element_type=F32)
    m = (jax.nn.sigmoid(ga_ref[...].astype(F32)) * ya
         + jax.nn.sigmoid(gb_ref[...].astype(F32)) * yb)
    x = x_ref[...] + jnp.dot(m.astype(BF16), wo_ref[...], preferred_element_type=F32)
    h = (_rms(x) * g2_ref[...]).astype(BF16)
    gate = jnp.dot(h, wg_ref[...], preferred_element_type=F32)
    up = jnp.dot(h, wu_ref[...], preferred_element_type=F32)
    act = (gate * jax.nn.sigmoid(gate) * up).astype(BF16)
    x = x + jnp.dot(act, wd_ref[...], preferred_element_type=F32)
    o_ref[...] = _rms(x) * gf_ref[...]


def _post(x2d, a, r, ga, gb, wa, wr, wo, g_ffn, g_final, wg, wu, wd):
    tokens = x2d.shape[0]
    tm = POST_TM
    row = lambda w: pl.BlockSpec((tm, w), lambda i: (i, 0))
    consts = [wa, wr, wo, g_ffn, g_final, wg, wu, wd]
    return pl.pallas_call(
        _post_kernel,
        grid=(tokens // tm,),
        in_specs=[row(D_MODEL), row(DIFF_W), row(RET_V_W), row(D_MODEL), row(D_MODEL)]
        + [_const_spec(c.shape) for c in consts],
        out_specs=row(D_MODEL),
        out_shape=jax.ShapeDtypeStruct((tokens, D_MODEL), F32),
        compiler_params=pltpu.CompilerParams(
            dimension_semantics=("parallel",), vmem_limit_bytes=VMEM_LIMIT),
        name="post",
    )(x2d, a, r, ga, gb, *consts)


def _rope_tables(seq):
    half = QK_DIM // 2
    inv_freq = ROPE_THETA ** (-jnp.arange(0, QK_DIM, 2, dtype=F32) / QK_DIM)
    ang = jnp.arange(seq, dtype=F32)[:, None] * inv_freq[None, :]
    cos, sin = jnp.cos(ang), jnp.sin(ang)
    cos_t = jnp.tile(cos, (1, LANES // half))
    sin_t = jnp.tile(jnp.concatenate([-sin, sin], axis=-1), (1, LANES // QK_DIM))
    return cos_t, sin_t


def _decay_table(fwd, bwd):
    pairs = HEADS // 2
    f = fwd.astype(F32).reshape(pairs, 2)
    b = bwd.astype(F32).reshape(pairs, 2)
    k_lanes = lambda t: jnp.tile(jnp.repeat(t, QK_DIM, axis=1), (1, 2))
    v_lanes = lambda t: jnp.repeat(t, V_DIM, axis=1)
    all_lanes = lambda col: jnp.broadcast_to(col[:, None], (pairs, 2 * V_DIM))
    rows = [k_lanes(f), k_lanes(b), v_lanes(f), v_lanes(b),
            all_lanes(f[:, 0]), all_lanes(b[:, 0]), all_lanes(f[:, 1]), all_lanes(b[:, 1])]
    return jnp.stack(rows, axis=1)


def kernel(x, g_mix, w_in, diff_lq1, diff_lk1, diff_lq2, diff_lk2, diff_subln_g,
           ret_decay_fwd, ret_decay_bwd, w_up_diff, w_up_ret, w_o,
           g_ffn, w_ffn_gate, w_ffn_up, w_ffn_down, g_final):
    batch, seq, _ = x.shape
    layer = 0
    x2d = x.reshape(batch * seq, D_MODEL)
    cos_t, sin_t = _rope_tables(seq)

    dq, dk, dv, rq, rk, rv, rg, ga, gb = _proj(
        x2d, g_mix[layer][None, :], w_in[layer].astype(BF16), cos_t, sin_t, batch, seq)

    lam_params = jnp.stack([diff_lq1[layer], diff_lk1[layer], diff_lq2[layer], diff_lk2[layer]])
    a = _attn(lam_params.astype(F32), diff_subln_g[layer][None, :].astype(F32), dq, dk, dv)

    r = _ret(_decay_table(ret_decay_fwd[layer], ret_decay_bwd[layer]), rq, rk, rv, rg)

    y = _post(x2d, a, r, ga, gb, w_up_diff[layer].astype(BF16), w_up_ret[layer].astype(BF16),
              w_o[layer].astype(BF16), g_ffn[layer][None, :], g_final[None, :],
              w_ffn_gate[layer].astype(BF16), w_ffn_up[layer].astype(BF16), w_ffn_down[layer].astype(BF16))
    return y.reshape(batch, seq, D_MODEL)
```

```python
import functools
import math

import jax
import jax.numpy as jnp
from jax import lax
from jax.experimental import pallas as pl
from jax.experimental.pallas import tpu as pltpu

F32 = jnp.float32
BF16 = jnp.bfloat16

D_MODEL = 1024
HEADS = 4
QK_DIM = 64
V_DIM = 128
DIFF_W = HEADS * 2 * QK_DIM
RET_QK_W = HEADS * QK_DIM
RET_V_W = HEADS * V_DIM
IN_COLS = 3 * DIFF_W + 2 * RET_QK_W + 2 * RET_V_W + 2 * D_MODEL
D_FF = 2816
ROPE_THETA = 10000.0
NORM_EPS = 1e-5
LAM_INIT = 0.8 - 0.6 * math.exp(-0.3 * 0)
LOG2E = 1.4426950408889634

LANES = 128
VMEM_LIMIT = 56 * 1024 * 1024
PROJ_TM = 512
PROJ_CAST_COLS = 512
ATTN_TQ = 256
ATTN_ONES_ROWS = 16
ATTN_KB = 512
ATTN_PB = 256
RET_CHUNK = 256
POST_TM = 512


def _rms(x, eps=NORM_EPS):
    return x * lax.rsqrt(jnp.mean(x * x, axis=-1, keepdims=True) + eps)


def _const_spec(shape):
    nd = len(shape)
    return pl.BlockSpec(shape, lambda *_: (0,) * nd, pipeline_mode=pl.Buffered(1))


def _proj_kernel(x_ref, g_ref, wf_ref, cos_ref, sin_ref,
                 dq_ref, dk_ref, dv_ref, rq_ref, rk_ref, rv_ref, rg_ref, ga_ref, gb_ref, wb_ref):
    tm = x_ref.shape[0]

    @pl.when(pl.program_id(0) == 0)
    def _():
        for c in range(0, IN_COLS, PROJ_CAST_COLS):
            wb_ref[:, c:c + PROJ_CAST_COLS] = wf_ref[:, c:c + PROJ_CAST_COLS].astype(BF16)

    hb =(_rms(x_ref[...]) * g_ref[...]).astype(BF16)
    cos = cos_ref[...]
    sin = sin_ref[...]
    lane = lax.broadcasted_iota(jnp.int32, (tm, LANES), 1)
    first_half = (lane % QK_DIM) < (QK_DIM // 2)

    def proj(c0, n):
        return jnp.dot(hb, wb_ref[:, c0:c0 + n], preferred_element_type=F32)

    def rope(y, scale):
        partner = jnp.where(first_half, pltpu.roll(y, LANES - 32, 1), pltpu.roll(y, 32, 1))
        return ((y * cos + partner * sin) * scale).astype(BF16)

    q_scale = QK_DIM ** -0.5 * LOG2E
    y = proj(0, DIFF_W)
    for h in range(HEADS):
        dq_ref[h] = rope(y[:, h * LANES:(h + 1) * LANES], q_scale)
    y = proj(DIFF_W, DIFF_W)
    for h in range(HEADS):
        dk_ref[h] = rope(y[:, h * LANES:(h + 1) * LANES], 1.0)
    y = proj(2 * DIFF_W, DIFF_W)
    for h in range(HEADS):
        dv_ref[h] = y[:, h * LANES:(h + 1) * LANES].astype(BF16)
    c0 = 3 * DIFF_W
    y = proj(c0, 2 * RET_QK_W)
    for p in range(HEADS // 2):
        rq_ref[p] = rope(y[:, p * LANES:(p + 1) * LANES], 1.0)
        rk_ref[p] = rope(y[:, RET_QK_W + p * LANES:RET_QK_W + (p + 1) * LANES], QK_DIM ** -0.5)
    c0 += 2 * RET_QK_W
    y = proj(c0, RET_V_W)
    for p in range(HEADS // 2):
        rv_ref[p] = y[:, p * 2 * V_DIM:(p + 1) * 2 * V_DIM].astype(BF16)
    c0 += RET_V_W
    y = proj(c0, RET_V_W)
    for p in range(HEADS // 2):
        rg_ref[p] = y[:, p * 2 * V_DIM:(p + 1) * 2 * V_DIM].astype(BF16)
    c0 += RET_V_W
    for half in range(2):
        ga_ref[:, half * 512:(half + 1) * 512] = proj(c0 + half * 512, 512).astype(BF16)
    c0 += D_MODEL
    for half in range(2):
        gb_ref[:, half * 512:(half + 1) * 512] = proj(c0 + half * 512, 512).astype(BF16)


def _proj(x2d, g_mix, w_in, cos_t, sin_t, batch, seq):
    tm = PROJ_TM
    nb = seq // tm
    tokens = batch * seq

    def head_major(n_groups, width):
        shape = jax.ShapeDtypeStruct((batch, n_groups, seq, width), BF16)
        spec = pl.BlockSpec((None, n_groups, tm, width), lambda i: (i // nb, 0, i % nb, 0))
        return shape, spec

    outs = [head_major(HEADS, LANES)] * 3 + [head_major(HEADS // 2, LANES)] * 2 \
        + [head_major(HEADS // 2, 2 * V_DIM)] * 2
    out_shape = [o[0] for o in outs] + [jax.ShapeDtypeStruct((tokens, D_MODEL), BF16)] * 2
    out_specs = [o[1] for o in outs] + [pl.BlockSpec((tm, D_MODEL), lambda i: (i, 0))] * 2
    return pl.pallas_call(
        _proj_kernel,
        grid=(tokens // tm,),
        in_specs=[
            pl.BlockSpec((tm, D_MODEL), lambda i: (i, 0)),
            _const_spec((1, D_MODEL)),
            _const_spec((D_MODEL, IN_COLS)),
            pl.BlockSpec((tm, LANES), lambda i: (i % nb, 0)),
            pl.BlockSpec((tm, LANES), lambda i: (i % nb, 0)),
        ],
        out_specs=out_specs,
        out_shape=out_shape,
        scratch_shapes=[pltpu.VMEM((D_MODEL, IN_COLS), BF16)],
        compiler_params=pltpu.CompilerParams(
            dimension_semantics=("arbitrary",), vmem_limit_bytes=VMEM_LIMIT),
        name="in_proj",
    )(x2d, g_mix, w_in, cos_t, sin_t)


def _attn_kernel(lam_ref, g_ref, q_ref, k_ref, v_ref, o_ref,
                 vext_ref, s0_ref, s1_ref, m0_ref, m1_ref, a0_ref, a1_ref):
    seq = q_ref.shape[0]
    tq = ATTN_TQ
    nq = seq // tq
    nt_dims = (((1,), (1,)), ((), ()))

    vext_ref[:V_DIM, :] = v_ref[...].astype(F32).T.astype(BF16)
    vext_ref[V_DIM:, :] = jnp.ones((ATTN_ONES_ROWS, seq), BF16)

    lp = lam_ref[...]
    lam = (jnp.exp(jnp.sum(lp[0:1] * lp[1:2], axis=-1, keepdims=True))
           - jnp.exp(jnp.sum(lp[2:3] * lp[3:4], axis=-1, keepdims=True)) + LAM_INIT)
    lane = lax.broadcasted_iota(jnp.int32, (tq, LANES), 1)

    bufs = ((s0_ref, m0_ref, a0_ref), (s1_ref, m1_ref, a1_ref))

    def step(qk=None, pv=None, fin=None):
        if fin is not None:
            j, (_, _, fa_ref) = fin
            acc = fa_ref[...]
            o = acc[:V_DIM] / acc[V_DIM:V_DIM + 1]
            o = (o[:, :tq] - lam * o[:, tq:]).T
            o = _rms(o) * g_ref[...] * (1.0 - LAM_INIT)
            o_ref[pl.ds(pl.multiple_of(j * tq, tq), tq), :] = o.astype(BF16)
        if qk is not None:
            i, (qs_ref, qm_ref, _) = qk
            q = q_ref[pl.ds(pl.multiple_of(i * tq, tq), tq), :]
            zero = jnp.zeros_like(q)
            q2 = jnp.concatenate([jnp.where(lane < QK_DIM, q, zero),
                                  jnp.where(lane >= QK_DIM, q, zero)], axis=0)
            m = None
        if pv is not None:
            _, (ps_ref, pm_ref, pa_ref) = pv
            m_prev = pm_ref[...]
            acc = None
        for kb in range(seq // ATTN_KB):
            rows = slice(kb * ATTN_KB, (kb + 1) * ATTN_KB)
            if qk is not None:
                s = lax.dot_general(k_ref[rows, :], q2, nt_dims, preferred_element_type=F32)
                qs_ref[rows, :] = s
                blk = jnp.max(s, axis=0, keepdims=True)
                m = blk if m is None else jnp.maximum(m, blk)
            if pv is not None:
                for sub in range(ATTN_KB // ATTN_PB):
                    r0 = kb * ATTN_KB + sub * ATTN_PB
                    rows2 = slice(r0, r0 + ATTN_PB)
                    p = jnp.exp2(ps_ref[rows2, :] - m_prev).astype(BF16)
                    part = jnp.dot(vext_ref[:, rows2], p, preferred_element_type=F32)
                    acc = part if acc is None else acc + part
        if qk is not None:
            qm_ref[...] = m
        if pv is not None:
            pa_ref[...] = acc

    def unit(t, parity):
        return t, bufs[parity]

    step(qk=unit(0, 0))
    step(qk=unit(1, 1), pv=unit(0, 0))

    def body(jj, carry):
        t = 2 * jj
        step(qk=unit(t, 0), pv=unit(t - 1, 1), fin=unit(t - 2, 0))
        step(qk=unit(t + 1, 1), pv=unit(t, 0), fin=unit(t - 1, 1))
        return carry

    lax.fori_loop(1, nq // 2, body, 0)

    step(pv=unit(nq - 1, 1), fin=unit(nq - 2, 0))
    step(fin=unit(nq - 1, 1))


def _attn(lam_params, g_sub, dq, dk, dv):
    batch, heads, seq, _ = dq.shape
    tq = ATTN_TQ
    assert (seq // tq) % 2 == 0
    seq_spec = pl.BlockSpec((None, None, seq, LANES), lambda b, h: (b, h, 0, 0))
    return pl.pallas_call(
        _attn_kernel,
        grid=(batch, heads),
        in_specs=[_const_spec((4, QK_DIM)), _const_spec((1, V_DIM)), seq_spec, seq_spec, seq_spec],
        out_specs=pl.BlockSpec((seq, V_DIM), lambda b, h: (b, h)),
        out_shape=jax.ShapeDtypeStruct((batch * seq, HEADS * V_DIM), BF16),
        scratch_shapes=[
            pltpu.VMEM((V_DIM + ATTN_ONES_ROWS, seq), BF16),
            pltpu.VMEM((seq, 2 * tq), F32), pltpu.VMEM((seq, 2 * tq), F32),
            pltpu.VMEM((1, 2 * tq), F32), pltpu.VMEM((1, 2 * tq), F32),
            pltpu.VMEM((V_DIM + ATTN_ONES_ROWS, 2 * tq), F32),
            pltpu.VMEM((V_DIM + ATTN_ONES_ROWS, 2 * tq), F32),
        ],
        compiler_params=pltpu.CompilerParams(
            dimension_semantics=("parallel", "parallel"), vmem_limit_bytes=VMEM_LIMIT),
        name="diff_attn",
    )(lam_params, g_sub, dq, dk, dv)


def _log_sigmoid(x):
    return jnp.minimum(x, 0.0) - jnp.log1p(jnp.exp(-jnp.abs(x)))


def _ret_kernel(dec_ref, q_ref, k_ref, v_ref, rg_ref, o_ref, dmat_ref, u_ref, sf_ref, sb_ref):
    seq = q_ref.shape[0]
    c_len = RET_CHUNK
    n_chunks = seq // c_len
    pair_w = 2 * V_DIM
    nt_dims = (((1,), (1,)), ((), ()))
    tn_dims = (((0,), (0,)), ((), ()))

    lg = _log_sigmoid(dec_ref[...])
    row_i = lax.broadcasted_iota(jnp.int32, (c_len, c_len), 0)
    col_i = lax.broadcasted_iota(jnp.int32, (c_len, c_len), 1)
    diff = (row_i - col_i).astype(F32)
    for hh in range(2):
        dmat_ref[hh * c_len:(hh + 1) * c_len, :] = jnp.exp(
            jnp.where(diff >= 0, diff * lg[4 + 2 * hh:5 + 2 * hh], -diff * lg[5 + 2 * hh:6 + 2 * hh]))
    rows_k = lax.broadcasted_iota(jnp.int32, (c_len, LANES), 0).astype(F32)
    rows_v = lax.broadcasted_iota(jnp.int32, (c_len, pair_w), 0).astype(F32)
    zeta_f = jnp.exp((c_len - 1 - rows_k) * lg[0:1, :LANES])
    zeta_b = jnp.exp(rows_k * lg[1:2, :LANES])
    xi_f = jnp.exp((rows_v + 1) * lg[2:3])
    xi_b = jnp.exp((c_len - rows_v) * lg[3:4])
    srow = lax.broadcasted_iota(jnp.int32, (LANES, pair_w), 0)
    scol = lax.broadcasted_iota(jnp.int32, (LANES, pair_w), 1)
    head_a_row = srow < QK_DIM
    same_head = head_a_row == (scol < V_DIM)
    g_f = jnp.exp(c_len * jnp.where(head_a_row, lg[4:5], lg[6:7]))
    g_b = jnp.exp(c_len * jnp.where(head_a_row, lg[5:6], lg[7:8]))
    lane = lax.broadcasted_iota(jnp.int32, (c_len, LANES), 1)

    def chunk(c):
        return pl.ds(pl.multiple_of(c * c_len, c_len), c_len)

    def phase_a(c, carry):
        kc = k_ref[chunk(c), :].astype(F32)
        kz = jnp.concatenate([kc * zeta_f, kc * zeta_b], axis=1).astype(BF16)
        u_ref[c] = lax.dot_general(kz, v_ref[chunk(c), :], tn_dims, preferred_element_type=F32)
        return carry

    lax.fori_loop(0, n_chunks, phase_a, 0, unroll=8)

    def phase_b(t, carry):
        s_f, s_b = carry
        sf_ref[t] = jnp.where(same_head, s_f, 0.0).astype(BF16)
        s_f = g_f * s_f + u_ref[t, :LANES, :]
        cb = n_chunks - 1 - t
        sb_ref[cb] = jnp.where(same_head, s_b, 0.0).astype(BF16)
        s_b = g_b * s_b + u_ref[cb, LANES:, :]
        return s_f, s_b

    zero_state = jnp.zeros((LANES, pair_w), F32)
    lax.fori_loop(0, n_chunks, phase_b, (zero_state, zero_state))

    def phase_c(c, carry):
        q = q_ref[chunk(c), :]
        zero = jnp.zeros_like(q)
        q2 = jnp.concatenate([jnp.where(lane < QK_DIM, q, zero),
                              jnp.where(lane >= QK_DIM, q, zero)], axis=0)
        vc = v_ref[chunk(c), :]
        sc = lax.dot_general(q2, k_ref[chunk(c), :], nt_dims, preferred_element_type=F32) * dmat_ref[...]
        inner = jnp.dot(sc.astype(BF16), vc, preferred_element_type=F32)
        tot = (jnp.concatenate([inner[:c_len, :V_DIM], inner[c_len:, V_DIM:]], axis=1)
               + jnp.dot(q, sf_ref[c], preferred_element_type=F32) * xi_f
               + jnp.dot(q, sb_ref[c], preferred_element_type=F32) * xi_b)
        r = jnp.concatenate([_rms(tot[:, :V_DIM]), _rms(tot[:, V_DIM:])], axis=1)
        gate = rg_ref[chunk(c), :].astype(F32)
        o_ref[chunk(c), :] = (r * gate * jax.nn.sigmoid(gate)).astype(BF16)
        return carry

    lax.fori_loop(0, n_chunks, phase_c, 0, unroll=8)


def _ret(dec, rq, rk, rv, rg):
    batch, pairs, seq, _ = rq.shape
    pair_w = 2 * V_DIM
    n_chunks = seq // RET_CHUNK
    assert RET_CHUNK == pair_w
    return pl.pallas_call(
        _ret_kernel,
        grid=(batch, pairs),
        in_specs=[
            pl.BlockSpec((None, 8, pair_w), lambda b, p: (p, 0, 0)),
            pl.BlockSpec((None, None, seq, LANES), lambda b, p: (b, p, 0, 0)),
            pl.BlockSpec((None, None, seq, LANES), lambda b, p: (b, p, 0, 0)),
            pl.BlockSpec((None, None, seq, pair_w), lambda b, p: (b, p, 0, 0)),
            pl.BlockSpec((None, None, seq, pair_w), lambda b, p: (b, p, 0, 0)),
        ],
        out_specs=pl.BlockSpec((seq, pair_w), lambda b, p: (b, p)),
        out_shape=jax.ShapeDtypeStruct((batch * seq, RET_V_W), BF16),
        scratch_shapes=[
            pltpu.VMEM((2 * RET_CHUNK, RET_CHUNK), F32),
            pltpu.VMEM((n_chunks, 2 * LANES, pair_w), F32),
            pltpu.VMEM((n_chunks, LANES, pair_w), BF16),
            pltpu.VMEM((n_chunks, LANES, pair_w), BF16),
        ],
        compiler_params=pltpu.CompilerParams(
            dimension_semantics=("parallel", "parallel"), vmem_limit_bytes=VMEM_LIMIT),
        name="retention",
    )(dec, rq, rk, rv, rg)


def _post_kernel(x_ref, a_ref, r_ref, ga_ref, gb_ref, wa_ref, wr_ref, wo_ref,
                 g2_ref, gf_ref, wg_ref, wu_ref, wd_ref, o_ref):
    ya = jnp.dot(a_ref[...], wa_ref[...], preferred_element_type=F32)
    yb = jnp.dot(r_ref[...], wr_ref[...], preferred_element_type=F32)
    m = (jax.nn.sigmoid(ga_ref[...].astype(F32)) * ya
         + jax.nn.sigmoid(gb_ref[...].astype(F32)) * yb)
    x = x_ref[...] + jnp.dot(m.astype(BF16), wo_ref[...], preferred_element_type=F32)
    h = (_rms(x) * g2_ref[...]).astype(BF16)
    gate = jnp.dot(h, wg_ref[...], preferred_element_type=F32)
    up = jnp.dot(h, wu_ref[...], preferred_element_type=F32)
    act = (gate * jax.nn.sigmoid(gate) * up).astype(BF16)
    x = x + jnp.dot(act, wd_ref[...], preferred_element_type=F32)
    o_ref[...] = _rms(x) * gf_ref[...]


def _post(x2d, a, r, ga, gb, wa, wr, wo, g_ffn, g_final, wg, wu, wd):
    tokens = x2d.shape[0]
    tm = POST_TM
    row = lambda w: pl.BlockSpec((tm, w), lambda i: (i, 0))
    consts = [wa, wr, wo, g_ffn, g_final, wg, wu, wd]
    return pl.pallas_call(
        _post_kernel,
        grid=(tokens // tm,),
        in_specs=[row(D_MODEL), row(DIFF_W), row(RET_V_W), row(D_MODEL), row(D_MODEL)]
        + [_const_spec(c.shape) for c in consts],
        out_specs=row(D_MODEL),
        out_shape=jax.ShapeDtypeStruct((tokens, D_MODEL), F32),
        compiler_params=pltpu.CompilerParams(
            dimension_semantics=("parallel",), vmem_limit_bytes=VMEM_LIMIT),
        name="post",
    )(x2d, a, r, ga, gb, *consts)


def _rope_tables(seq):
    half = QK_DIM // 2
    inv_freq = ROPE_THETA ** (-jnp.arange(0, QK_DIM, 2, dtype=F32) / QK_DIM)
    ang = jnp.arange(seq, dtype=F32)[:, None] * inv_freq[None, :]
    cos, sin = jnp.cos(ang), jnp.sin(ang)
    cos_t = jnp.tile(cos, (1, LANES // half))
    sin_t = jnp.tile(jnp.concatenate([-sin, sin], axis=-1), (1, LANES // QK_DIM))
    return cos_t, sin_t


def _decay_table(fwd, bwd):
    pairs = HEADS // 2
    f = fwd.astype(F32).reshape(pairs, 2)
    b = bwd.astype(F32).reshape(pairs, 2)
    k_lanes = lambda t: jnp.tile(jnp.repeat(t, QK_DIM, axis=1), (1, 2))
    v_lanes = lambda t: jnp.repeat(t, V_DIM, axis=1)
    all_lanes = lambda col: jnp.broadcast_to(col[:, None], (pairs, 2 * V_DIM))
    rows = [k_lanes(f), k_lanes(b), v_lanes(f), v_lanes(b),
            all_lanes(f[:, 0]), all_lanes(b[:, 0]), all_lanes(f[:, 1]), all_lanes(b[:, 1])]
    return jnp.stack(rows, axis=1)


def kernel(x, g_mix, w_in, diff_lq1, diff_lk1, diff_lq2, diff_lk2, diff_subln_g,
           ret_decay_fwd, ret_decay_bwd, w_up_diff, w_up_ret, w_o,
           g_ffn, w_ffn_gate, w_ffn_up, w_ffn_down, g_final):
    batch, seq, _ = x.shape
    layer = 0
    x2d = x.reshape(batch * seq, D_MODEL)
    cos_t, sin_t = _rope_tables(seq)

    dq, dk, dv, rq, rk, rv, rg, ga, gb = _proj(
        x2d, g_mix[layer][None, :], w_in[layer], cos_t, sin_t, batch, seq)

    lam_params = jnp.stack([diff_lq1[layer], diff_lk1[layer], diff_lq2[layer], diff_lk2[layer]])
    a = _attn(lam_params.astype(F32), diff_subln_g[layer][None, :].astype(F32), dq, dk, dv)

    r = _ret(_decay_table(ret_decay_fwd[layer], ret_decay_bwd[layer]), rq, rk, rv, rg)

    y = _post(x2d, a, r, ga, gb, w_up_diff[layer].astype(BF16), w_up_ret[layer].astype(BF16),
              w_o[layer].astype(BF16), g_ffn[layer][None, :], g_final[None, :],
              w_ffn_gate[layer].astype(BF16), w_ffn_up[layer].astype(BF16), w_ffn_down[layer].astype(BF16))
    return y.reshape(batch, seq, D_MODEL)
```

```python
import functools
import math

import jax
import jax.numpy as jnp
from jax import lax
from jax.experimental import pallas as pl
from jax.experimental.pallas import tpu as pltpu

F32 = jnp.float32
BF16 = jnp.bfloat16

D_MODEL = 1024
HEADS = 4
QK_DIM = 64
V_DIM = 128
DIFF_W = HEADS * 2 * QK_DIM
RET_QK_W = HEADS * QK_DIM
RET_V_W = HEADS * V_DIM
IN_COLS = 3 * DIFF_W + 2 * RET_QK_W + 2 * RET_V_W + 2 * D_MODEL
D_FF = 2816
ROPE_THETA = 10000.0
NORM_EPS = 1e-5
LAM_INIT = 0.8 - 0.6 * math.exp(-0.3 * 0)
LOG2E = 1.4426950408889634

LANES = 128
VMEM_LIMIT = 56 * 1024 * 1024
PROJ_TM = 512
PROJ_CAST_COLS = 512
N_LATER_WEIGHTS = 6
BF16_SUBLANES = 16
ATTN_TQ = 256
ATTN_ONES_ROWS = 16
ATTN_KB = 512
ATTN_PB = 256
RET_CHUNK = 256
POST_TM = 512


def _rms(x, eps=NORM_EPS):
    return x * lax.rsqrt(jnp.mean(x * x, axis=-1, keepdims=True) + eps)


def _const_spec(shape):
    nd = len(shape)
    return pl.BlockSpec(shape, lambda *_: (0,) * nd, pipeline_mode=pl.Buffered(1))


def _proj_kernel(x_ref, g_ref, wf_ref, cos_ref, sin_ref, *refs):
    later_f32 = refs[:N_LATER_WEIGHTS]
    dq_ref, dk_ref, dv_ref, rq_ref, rk_ref, rv_ref, rg_ref, ga_ref, gb_ref = refs[N_LATER_WEIGHTS:-N_LATER_WEIGHTS - 1]
    later_bf16 = refs[-N_LATER_WEIGHTS - 1:-1]
    wb_ref = refs[-1]
    tm = x_ref.shape[0]

    @pl.when(pl.program_id(0) == 0)
    def _():
        for c in range(0, IN_COLS, PROJ_CAST_COLS):
            wb_ref[:, c:c + PROJ_CAST_COLS] = wf_ref[:, c:c + PROJ_CAST_COLS].astype(BF16)

    for src_ref, dst_ref in zip(later_f32, later_bf16):
        dst_ref[...] = src_ref[...].astype(BF16)

    hb = (_rms(x_ref[...]) * g_ref[...]).astype(BF16)
    cos = cos_ref[...]
    sin = sin_ref[...]
    lane = lax.broadcasted_iota(jnp.int32, (tm, LANES), 1)
    first_half = (lane % QK_DIM) < (QK_DIM // 2)

    def proj(c0, n):
        return jnp.dot(hb, wb_ref[:, c0:c0 + n], preferred_element_type=F32)

    def rope(y, scale):
        partner = jnp.where(first_half, pltpu.roll(y, LANES - 32, 1), pltpu.roll(y, 32, 1))
        return ((y * cos + partner * sin) * scale).astype(BF16)

    q_scale = QK_DIM ** -0.5 * LOG2E
    y = proj(0, DIFF_W)
    for h in range(HEADS):
        dq_ref[h] = rope(y[:, h * LANES:(h + 1) * LANES], q_scale)
    y = proj(DIFF_W, DIFF_W)
    for h in range(HEADS):
        dk_ref[h] = rope(y[:, h * LANES:(h + 1) * LANES], 1.0)
    y = proj(2 * DIFF_W, DIFF_W)
    for h in range(HEADS):
        dv_ref[h] = y[:, h * LANES:(h + 1) * LANES].astype(BF16)
    c0 = 3 * DIFF_W
    y = proj(c0, 2 * RET_QK_W)
    for p in range(HEADS // 2):
        rq_ref[p] = rope(y[:, p * LANES:(p + 1) * LANES], 1.0)
        rk_ref[p] = rope(y[:, RET_QK_W + p * LANES:RET_QK_W + (p + 1) * LANES], QK_DIM ** -0.5)
    c0 += 2 * RET_QK_W
    y = proj(c0, RET_V_W)
    for p in range(HEADS // 2):
        rv_ref[p] = y[:, p * 2 * V_DIM:(p + 1) * 2 * V_DIM].astype(BF16)
    c0 += RET_V_W
    y = proj(c0, RET_V_W)
    for p in range(HEADS // 2):
        rg_ref[p] = y[:, p * 2 * V_DIM:(p + 1) * 2 * V_DIM].astype(BF16)
    c0 += RET_V_W
    for half in range(2):
        ga_ref[:, half * 512:(half + 1) * 512] = proj(c0 + half * 512, 512).astype(BF16)
    c0 += D_MODEL
    for half in range(2):
        gb_ref[:, half * 512:(half + 1) * 512] = proj(c0 + half * 512, 512).astype(BF16)


def _proj(x2d, g_mix, w_in, cos_t, sin_t, later_weights, batch, seq):
    tm = PROJ_TM
    nb = seq // tm
    tokens = batch * seq
    steps = tokens // tm
    assert len(later_weights) == N_LATER_WEIGHTS
    slab_specs = []
    for w in later_weights:
        slab = w.shape[0] // steps
        assert slab * steps == w.shape[0] and slab % BF16_SUBLANES == 0, w.shape
        slab_specs.append(pl.BlockSpec((slab, w.shape[1]), lambda i: (i, 0)))

    def head_major(n_groups, width):
        shape = jax.ShapeDtypeStruct((batch, n_groups, seq, width), BF16)
        spec = pl.BlockSpec((None, n_groups, tm, width), lambda i: (i // nb, 0, i % nb, 0))
        return shape, spec

    outs = [head_major(HEADS, LANES)] * 3 + [head_major(HEADS // 2, LANES)] * 2 \
        + [head_major(HEADS // 2, 2 * V_DIM)] * 2
    out_shape = [o[0] for o in outs] + [jax.ShapeDtypeStruct((tokens, D_MODEL), BF16)] * 2 \
        + [jax.ShapeDtypeStruct(w.shape, BF16) for w in later_weights]
    out_specs = [o[1] for o in outs] + [pl.BlockSpec((tm, D_MODEL), lambda i: (i, 0))] * 2 + slab_specs
    return pl.pallas_call(
        _proj_kernel,
        grid=(steps,),
        in_specs=[
            pl.BlockSpec((tm, D_MODEL), lambda i: (i, 0)),
            _const_spec((1, D_MODEL)),
            _const_spec((D_MODEL, IN_COLS)),
            pl.BlockSpec((tm, LANES), lambda i: (i % nb, 0)),
            pl.BlockSpec((tm, LANES), lambda i: (i % nb, 0)),
        ] + slab_specs,
        out_specs=out_specs,
        out_shape=out_shape,
        scratch_shapes=[pltpu.VMEM((D_MODEL, IN_COLS), BF16)],
        compiler_params=pltpu.CompilerParams(
            dimension_semantics=("arbitrary",), vmem_limit_bytes=VMEM_LIMIT),
        name="in_proj",
    )(x2d, g_mix, w_in, cos_t, sin_t, *later_weights)


def _attn_kernel(lam_ref, g_ref, q_ref, k_ref, v_ref, o_ref,
                 vext_ref, s0_ref, s1_ref, m0_ref, m1_ref, a0_ref, a1_ref):
    seq = q_ref.shape[0]
    tq = ATTN_TQ
    nq = seq // tq
    nt_dims = (((1,), (1,)), ((), ()))

    vext_ref[:V_DIM, :] = v_ref[...].astype(F32).T.astype(BF16)
    vext_ref[V_DIM:, :] = jnp.ones((ATTN_ONES_ROWS, seq), BF16)

    lp = lam_ref[...]
    lam = (jnp.exp(jnp.sum(lp[0:1] * lp[1:2], axis=-1, keepdims=True))
           - jnp.exp(jnp.sum(lp[2:3] * lp[3:4], axis=-1, keepdims=True)) + LAM_INIT)
    lane = lax.broadcasted_iota(jnp.int32, (tq, LANES), 1)

    bufs = ((s0_ref, m0_ref, a0_ref), (s1_ref, m1_ref, a1_ref))

    def step(qk=None, pv=None, fin=None):
        if fin is not None:
            j, (_, _, fa_ref) = fin
            acc = fa_ref[...]
            o = acc[:V_DIM] / acc[V_DIM:V_DIM + 1]
            o = (o[:, :tq] - lam * o[:, tq:]).T
            o = _rms(o) * g_ref[...] * (1.0 - LAM_INIT)
            o_ref[pl.ds(pl.multiple_of(j * tq, tq), tq), :] = o.astype(BF16)
        if qk is not None:
            i, (qs_ref, qm_ref, _) = qk
            q = q_ref[pl.ds(pl.multiple_of(i * tq, tq), tq), :]
            zero = jnp.zeros_like(q)
            q2 = jnp.concatenate([jnp.where(lane < QK_DIM, q, zero),
                                  jnp.where(lane >= QK_DIM, q, zero)], axis=0)
            m = None
        if pv is not None:
            _, (ps_ref, pm_ref, pa_ref) = pv
            m_prev = pm_ref[...]
            acc = None
        for kb in range(seq // ATTN_KB):
            rows = slice(kb * ATTN_KB, (kb + 1) * ATTN_KB)
            if qk is not None:
                s = lax.dot_general(k_ref[rows, :], q2, nt_dims, preferred_element_type=F32)
                qs_ref[rows, :] = s
                blk = jnp.max(s, axis=0, keepdims=True)
                m = blk if m is None else jnp.maximum(m, blk)
            if pv is not None:
                for sub in range(ATTN_KB // ATTN_PB):
                    r0 = kb * ATTN_KB + sub * ATTN_PB
                    rows2 = slice(r0, r0 + ATTN_PB)
                    p = jnp.exp2(ps_ref[rows2, :] - m_prev).astype(BF16)
                    part = jnp.dot(vext_ref[:, rows2], p, preferred_element_type=F32)
                    acc = part if acc is None else acc + part
        if qk is not None:
            qm_ref[...] = m
        if pv is not None:
            pa_ref[...] = acc

    def unit(t, parity):
        return t, bufs[parity]

    step(qk=unit(0, 0))
    step(qk=unit(1, 1), pv=unit(0, 0))

    def body(jj, carry):
        t = 2 * jj
        step(qk=unit(t, 0), pv=unit(t - 1, 1), fin=unit(t - 2, 0))
        step(qk=unit(t + 1, 1), pv=unit(t, 0), fin=unit(t - 1, 1))
        return carry

    lax.fori_loop(1, nq // 2, body, 0)

    step(pv=unit(nq - 1, 1), fin=unit(nq - 2, 0))
    step(fin=unit(nq - 1, 1))


def _attn(lam_params, g_sub, dq, dk, dv):
    batch, heads, seq, _ = dq.shape
    tq = ATTN_TQ
    assert (seq // tq) % 2 == 0
    seq_spec = pl.BlockSpec((None, None, seq, LANES), lambda b, h: (b, h, 0, 0))
    return pl.pallas_call(
        _attn_kernel,
        grid=(batch, heads),
        in_specs=[_const_spec((4, QK_DIM)), _const_spec((1, V_DIM)), seq_spec, seq_spec, seq_spec],
        out_specs=pl.BlockSpec((seq, V_DIM), lambda b, h: (b, h)),
        out_shape=jax.ShapeDtypeStruct((batch * seq, HEADS * V_DIM), BF16),
        scratch_shapes=[
            pltpu.VMEM((V_DIM + ATTN_ONES_ROWS, seq), BF16),
            pltpu.VMEM((seq, 2 * tq), F32), pltpu.VMEM((seq, 2 * tq), F32),
            pltpu.VMEM((1, 2 * tq), F32), pltpu.VMEM((1, 2 * tq), F32),
            pltpu.VMEM((V_DIM + ATTN_ONES_ROWS, 2 * tq), F32),
            pltpu.VMEM((V_DIM + ATTN_ONES_ROWS, 2 * tq), F32),
        ],
        compiler_params=pltpu.CompilerParams(
            dimension_semantics=("parallel", "parallel"), vmem_limit_bytes=VMEM_LIMIT),
        name="diff_attn",
    )(lam_params, g_sub, dq, dk, dv)


def _log_sigmoid(x):
    return jnp.minimum(x, 0.0) - jnp.log1p(jnp.exp(-jnp.abs(x)))


def _ret_kernel(dec_ref, q_ref, k_ref, v_ref, rg_ref, o_ref, dmat_ref, u_ref, sf_ref, sb_ref):
    seq = q_ref.shape[0]
    c_len = RET_CHUNK
    n_chunks = seq // c_len
    pair_w = 2 * V_DIM
    nt_dims = (((1,), (1,)), ((), ()))
    tn_dims = (((0,), (0,)), ((), ()))

    lg = _log_sigmoid(dec_ref[...])
    row_i = lax.broadcasted_iota(jnp.int32, (c_len, c_len), 0)
    col_i = lax.broadcasted_iota(jnp.int32, (c_len, c_len), 1)
    diff = (row_i - col_i).astype(F32)
    for hh in range(2):
        dmat_ref[hh * c_len:(hh + 1) * c_len, :] = jnp.exp(
            jnp.where(diff >= 0, diff * lg[4 + 2 * hh:5 + 2 * hh], -diff * lg[5 + 2 * hh:6 + 2 * hh]))
    rows_k = lax.broadcasted_iota(jnp.int32, (c_len, LANES), 0).astype(F32)
    rows_v = lax.broadcasted_iota(jnp.int32, (c_len, pair_w), 0).astype(F32)
    zeta_f = jnp.exp((c_len - 1 - rows_k) * lg[0:1, :LANES])
    zeta_b = jnp.exp(rows_k * lg[1:2, :LANES])
    xi_f = jnp.exp((rows_v + 1) * lg[2:3])
    xi_b = jnp.exp((c_len - rows_v) * lg[3:4])
    srow = lax.broadcasted_iota(jnp.int32, (LANES, pair_w), 0)
    scol = lax.broadcasted_iota(jnp.int32, (LANES, pair_w), 1)
    head_a_row = srow < QK_DIM
    same_head = head_a_row == (scol < V_DIM)
    g_f = jnp.exp(c_len * jnp.where(head_a_row, lg[4:5], lg[6:7]))
    g_b = jnp.exp(c_len * jnp.where(head_a_row, lg[5:6], lg[7:8]))
    lane = lax.broadcasted_iota(jnp.int32, (c_len, LANES), 1)

    def chunk(c):
        return pl.ds(pl.multiple_of(c * c_len, c_len), c_len)

    def phase_a(c, carry):
        kc = k_ref[chunk(c), :].astype(F32)
        kz = jnp.concatenate([kc * zeta_f, kc * zeta_b], axis=1).astype(BF16)
        u_ref[c] = lax.dot_general(kz, v_ref[chunk(c), :], tn_dims, preferred_element_type=F32)
        return carry

    lax.fori_loop(0, n_chunks, phase_a, 0, unroll=8)

    def phase_b(t, carry):
        s_f, s_b = carry
        sf_ref[t] = jnp.where(same_head, s_f, 0.0).astype(BF16)
        s_f = g_f * s_f + u_ref[t, :LANES, :]
        cb = n_chunks - 1 - t
        sb_ref[cb] = jnp.where(same_head, s_b, 0.0).astype(BF16)
        s_b = g_b * s_b + u_ref[cb, LANES:, :]
        return s_f, s_b

    zero_state = jnp.zeros((LANES, pair_w), F32)
    lax.fori_loop(0, n_chunks, phase_b, (zero_state, zero_state))

    def phase_c(c, carry):
        q = q_ref[chunk(c), :]
        zero = jnp.zeros_like(q)
        q2 = jnp.concatenate([jnp.where(lane < QK_DIM, q, zero),
                              jnp.where(lane >= QK_DIM, q, zero)], axis=0)
        vc = v_ref[chunk(c), :]
        sc = lax.dot_general(q2, k_ref[chunk(c), :], nt_dims, preferred_element_type=F32) * dmat_ref[...]
        inner = jnp.dot(sc.astype(BF16), vc, preferred_element_type=F32)
        tot = (jnp.concatenate([inner[:c_len, :V_DIM], inner[c_len:, V_DIM:]], axis=1)
               + jnp.dot(q, sf_ref[c], preferred_element_type=F32) * xi_f
               + jnp.dot(q, sb_ref[c], preferred_element_type=F32) * xi_b)
        r = jnp.concatenate([_rms(tot[:, :V_DIM]), _rms(tot[:, V_DIM:])], axis=1)
        gate = rg_ref[chunk(c), :].astype(F32)
        o_ref[chunk(c), :] = (r * gate * jax.nn.sigmoid(gate)).astype(BF16)
        return carry

    lax.fori_loop(0, n_chunks, phase_c, 0, unroll=8)


def _ret(dec, rq, rk, rv, rg):
    batch, pairs, seq, _ = rq.shape
    pair_w = 2 * V_DIM
    n_chunks = seq // RET_CHUNK
    assert RET_CHUNK == pair_w
    return pl.pallas_call(
        _ret_kernel,
        grid=(batch, pairs),
        in_specs=[
            pl.BlockSpec((None, 8, pair_w), lambda b, p: (p, 0, 0)),
            pl.BlockSpec((None, None, seq, LANES), lambda b, p: (b, p, 0, 0)),
            pl.BlockSpec((None, None, seq, LANES), lambda b, p: (b, p, 0, 0)),
            pl.BlockSpec((None, None, seq, pair_w), lambda b, p: (b, p, 0, 0)),
            pl.BlockSpec((None, None, seq, pair_w), lambda b, p: (b, p, 0, 0)),
        ],
        out_specs=pl.BlockSpec((seq, pair_w), lambda b, p: (b, p)),
        out_shape=jax.ShapeDtypeStruct((batch * seq, RET_V_W), BF16),
        scratch_shapes=[
            pltpu.VMEM((2 * RET_CHUNK, RET_CHUNK), F32),
            pltpu.VMEM((n_chunks, 2 * LANES, pair_w), F32),
            pltpu.VMEM((n_chunks, LANES, pair_w), BF16),
            pltpu.VMEM((n_chunks, LANES, pair_w), BF16),
        ],
        compiler_params=pltpu.CompilerParams(
            dimension_semantics=("parallel", "parallel"), vmem_limit_bytes=VMEM_LIMIT),
        name="retention",
    )(dec, rq, rk, rv, rg)


def _post_kernel(x_ref, a_ref, r_ref, ga_ref, gb_ref, wa_ref, wr_ref, wo_ref,
                 g2_ref, gf_ref, wg_ref, wu_ref, wd_ref, o_ref):
    ya = jnp.dot(a_ref[...], wa_ref[...], preferred_element_type=F32)
    yb = jnp.dot(r_ref[...], wr_ref[...], preferred_element_type=F32)
    m = (jax.nn.sigmoid(ga_ref[...].astype(F32)) * ya
         + jax.nn.sigmoid(gb_ref[...].astype(F32)) * yb)
    x = x_ref[...] + jnp.dot(m.astype(BF16), wo_ref[...], preferred_element_type=F32)
    h = (_rms(x) * g2_ref[...]).astype(BF16)
    gate = jnp.dot(h, wg_ref[...], preferred_element_type=F32)
    up = jnp.dot(h, wu_ref[...], preferred_element_type=F32)
    act = (gate * jax.nn.sigmoid(gate) * up).astype(BF16)
    x = x + jnp.dot(act, wd_ref[...], preferred_element_type=F32)
    o_ref[...] = _rms(x) * gf_ref[...]


def _post(x2d, a, r, ga, gb, wa, wr, wo, g_ffn, g_final, wg, wu, wd):
    tokens = x2d.shape[0]
    tm = POST_TM
    row = lambda w: pl.BlockSpec((tm, w), lambda i: (i, 0))
    consts = [wa, wr, wo, g_ffn, g_final, wg, wu, wd]
    return pl.pallas_call(
        _post_kernel,
        grid=(tokens // tm,),
        in_specs=[row(D_MODEL), row(DIFF_W), row(RET_V_W), row(D_MODEL), row(D_MODEL)]
        + [_const_spec(c.shape) for c in consts],
        out_specs=row(D_MODEL),
        out_shape=jax.ShapeDtypeStruct((tokens, D_MODEL), F32),
        compiler_params=pltpu.CompilerParams(
            dimension_semantics=("parallel",), vmem_limit_bytes=VMEM_LIMIT),
        name="post",
    )(x2d, a, r, ga, gb, *consts)


def _rope_tables(seq):
    half = QK_DIM // 2
    inv_freq = ROPE_THETA ** (-jnp.arange(0, QK_DIM, 2, dtype=F32) / QK_DIM)
    ang = jnp.arange(seq, dtype=F32)[:, None] * inv_freq[None, :]
    cos, sin = jnp.cos(ang), jnp.sin(ang)
    cos_t = jnp.tile(cos, (1, LANES // half))
    sin_t = jnp.tile(jnp.concatenate([-sin, sin], axis=-1), (1, LANES // QK_DIM))
    return cos_t, sin_t


def _decay_table(fwd, bwd):
    pairs = HEADS // 2
    f = fwd.astype(F32).reshape(pairs, 2)
    b = bwd.astype(F32).reshape(pairs, 2)
    k_lanes = lambda t: jnp.tile(jnp.repeat(t, QK_DIM, axis=1), (1, 2))
    v_lanes = lambda t: jnp.repeat(t, V_DIM, axis=1)
    all_lanes = lambda col: jnp.broadcast_to(col[:, None], (pairs, 2 * V_DIM))
    rows = [k_lanes(f), k_lanes(b), v_lanes(f), v_lanes(b),
            all_lanes(f[:, 0]), all_lanes(b[:, 0]), all_lanes(f[:, 1]), all_lanes(b[:, 1])]
    return jnp.stack(rows, axis=1)


def kernel(x, g_mix, w_in, diff_lq1, diff_lk1, diff_lq2, diff_lk2, diff_subln_g,
           ret_decay_fwd, ret_decay_bwd, w_up_diff, w_up_ret, w_o,
           g_ffn, w_ffn_gate, w_ffn_up, w_ffn_down, g_final):
    batch, seq, _ = x.shape
    layer = 0
    x2d = x.reshape(batch * seq, D_MODEL)
    cos_t, sin_t = _rope_tables(seq)

    later = [w_up_diff[layer], w_up_ret[layer], w_o[layer], w_ffn_gate[layer], w_ffn_up[layer],
             w_ffn_down[layer].reshape(D_MODEL, D_FF)]
    (dq, dk, dv, rq, rk, rv, rg, ga, gb, wa, wr, wo, wg, wu, wd) = _proj(
        x2d, g_mix[layer][None, :], w_in[layer], cos_t, sin_t, later, batch, seq)
    wd = wd.reshape(D_FF, D_MODEL)

    lam_params = jnp.stack([diff_lq1[layer], diff_lk1[layer], diff_lq2[layer], diff_lk2[layer]])
    a = _attn(lam_params.astype(F32), diff_subln_g[layer][None, :].astype(F32), dq, dk, dv)

    r = _ret(_decay_table(ret_decay_fwd[layer], ret_decay_bwd[layer]), rq, rk, rv, rg)

    y = _post(x2d, a, r, ga, gb, wa, wr, wo, g_ffn[layer][None, :], g_final[None, :], wg, wu, wd)
    return y.reshape(batch, seq, D_MODEL)
```

```python
import functools
import math

import jax
import jax.numpy as jnp
from jax import lax
from jax.experimental import pallas as pl
from jax.experimental.pallas import tpu as pltpu

F32 = jnp.float32
BF16 = jnp.bfloat16

D_MODEL = 1024
HEADS = 4
QK_DIM = 64
V_DIM = 128
DIFF_W = HEADS * 2 * QK_DIM
RET_QK_W = HEADS * QK_DIM
RET_V_W = HEADS * V_DIM
IN_COLS = 3 * DIFF_W + 2 * RET_QK_W + 2 * RET_V_W + 2 * D_MODEL
D_FF = 2816
ROPE_THETA = 10000.0
NORM_EPS = 1e-5
LAM_INIT = 0.8 - 0.6 * math.exp(-0.3 * 0)
LOG2E = 1.4426950408889634

LANES = 128
VMEM_LIMIT = 56 * 1024 * 1024
PROJ_TM = 512
PROJ_CAST_COLS = 512
N_LATER_WEIGHTS = 6
BF16_SUBLANES = 16
ATTN_TQ = 256
ATTN_ONES_ROWS = 16
ATTN_KB = 512
ATTN_PB = 256
RET_CHUNK = 256
POST_TM = 512


def _rms(x, eps=NORM_EPS):
    return x * lax.rsqrt(jnp.mean(x * x, axis=-1, keepdims=True) + eps)


def _const_spec(shape):
    nd = len(shape)
    return pl.BlockSpec(shape, lambda *_: (0,) * nd, pipeline_mode=pl.Buffered(1))


def _proj_kernel(x_ref, g_ref, wf_ref, cos_ref, sin_ref, *refs):
    later_f32 = refs[:N_LATER_WEIGHTS]
    dq_ref, dk_ref, dv_ref, rq_ref, rk_ref, rv_ref, rg_ref, ga_ref, gb_ref = refs[N_LATER_WEIGHTS:-N_LATER_WEIGHTS - 1]
    later_bf16 = refs[-N_LATER_WEIGHTS - 1:-1]
    wb_ref = refs[-1]
    tm = x_ref.shape[0]

    @pl.when(pl.program_id(0) == 0)
    def _():
        for c in range(0, IN_COLS, PROJ_CAST_COLS):
            wb_ref[:, c:c + PROJ_CAST_COLS] = wf_ref[:, c:c + PROJ_CAST_COLS].astype(BF16)

    for src_ref, dst_ref in zip(later_f32, later_bf16):
        dst_ref[...] = src_ref[...].astype(BF16)

    hb = (_rms(x_ref[...]) * g_ref[...]).astype(BF16)
    cos = cos_ref[...]
    sin = sin_ref[...]
    lane = lax.broadcasted_iota(jnp.int32, (tm, LANES), 1)
    first_half = (lane % QK_DIM) < (QK_DIM // 2)

    def proj(c0, n):
        return jnp.dot(hb, wb_ref[:, c0:c0 + n], preferred_element_type=F32)

    def rope(y, scale):
        partner = jnp.where(first_half, pltpu.roll(y, LANES - 32, 1), pltpu.roll(y, 32, 1))
        return ((y * cos + partner * sin) * scale).astype(BF16)

    q_scale = QK_DIM ** -0.5 * LOG2E
    y = proj(0, DIFF_W)
    for h in range(HEADS):
        dq_ref[h] = rope(y[:, h * LANES:(h + 1) * LANES], q_scale)
    y = proj(DIFF_W, DIFF_W)
    for h in range(HEADS):
        dk_ref[h] = rope(y[:, h * LANES:(h + 1) * LANES], 1.0)
    y = proj(2 * DIFF_W, DIFF_W)
    for h in range(HEADS):
        dv_ref[h] = y[:, h * LANES:(h + 1) * LANES].astype(BF16)
    c0 = 3 * DIFF_W
    y = proj(c0, 2 * RET_QK_W)
    for p in range(HEADS // 2):
        rq_ref[p] = rope(y[:, p * LANES:(p + 1) * LANES], 1.0)
        rk_ref[p] = rope(y[:, RET_QK_W + p * LANES:RET_QK_W + (p + 1) * LANES], QK_DIM ** -0.5)
    c0 += 2 * RET_QK_W
    y = proj(c0, RET_V_W)
    for p in range(HEADS // 2):
        rv_ref[p] = y[:, p * 2 * V_DIM:(p + 1) * 2 * V_DIM].astype(BF16)
    c0 += RET_V_W
    y = proj(c0, RET_V_W)
    for p in range(HEADS // 2):
        rg_ref[p] = y[:, p * 2 * V_DIM:(p + 1) * 2 * V_DIM].astype(BF16)
    c0 += RET_V_W
    for half in range(2):
        ga_ref[:, half * 512:(half + 1) * 512] = proj(c0 + half * 512, 512).astype(BF16)
    c0 += D_MODEL
    for half in range(2):
        gb_ref[:, half * 512:(half + 1) * 512] = proj(c0 + half * 512, 512).astype(BF16)


def _proj(x2d, g_mix, w_in, cos_t, sin_t, later_weights, batch, seq):
    tm = PROJ_TM
    nb = seq // tm
    tokens = batch * seq
    steps = tokens // tm
    assert len(later_weights) == N_LATER_WEIGHTS
    slab_specs = []
    for w in later_weights:
        n_slabs = max(n for n in range(1, steps + 1)
                      if w.shape[0] % n == 0 and (w.shape[0] // n) % BF16_SUBLANES == 0)
        slab_specs.append(pl.BlockSpec((w.shape[0] // n_slabs, w.shape[1]),
                                       lambda i, last=n_slabs - 1: (jnp.minimum(i, last), 0)))

    def head_major(n_groups, width):
        shape = jax.ShapeDtypeStruct((batch, n_groups, seq, width), BF16)
        spec = pl.BlockSpec((None, n_groups, tm, width), lambda i: (i // nb, 0, i % nb, 0))
        return shape, spec

    outs = [head_major(HEADS, LANES)] * 3 + [head_major(HEADS // 2, LANES)] * 2 \
        + [head_major(HEADS // 2, 2 * V_DIM)] * 2
    out_shape = [o[0] for o in outs] + [jax.ShapeDtypeStruct((tokens, D_MODEL), BF16)] * 2 \
        + [jax.ShapeDtypeStruct(w.shape, BF16) for w in later_weights]
    out_specs = [o[1] for o in outs] + [pl.BlockSpec((tm, D_MODEL), lambda i: (i, 0))] * 2 + slab_specs
    return pl.pallas_call(
        _proj_kernel,
        grid=(steps,),
        in_specs=[
            pl.BlockSpec((tm, D_MODEL), lambda i: (i, 0)),
            _const_spec((1, D_MODEL)),
            _const_spec((D_MODEL, IN_COLS)),
            pl.BlockSpec((tm, LANES), lambda i: (i % nb, 0)),
            pl.BlockSpec((tm, LANES), lambda i: (i % nb, 0)),
        ] + slab_specs,
        out_specs=out_specs,
        out_shape=out_shape,
        scratch_shapes=[pltpu.VMEM((D_MODEL, IN_COLS), BF16)],
        compiler_params=pltpu.CompilerParams(
            dimension_semantics=("arbitrary",), vmem_limit_bytes=VMEM_LIMIT),
        name="in_proj",
    )(x2d, g_mix, w_in, cos_t, sin_t, *later_weights)


def _attn_kernel(lam_ref, g_ref, q_ref, k_ref, v_ref, o_ref,
                 vext_ref, s0_ref, s1_ref, m0_ref, m1_ref, a0_ref, a1_ref):
    seq = q_ref.shape[0]
    tq = ATTN_TQ
    nq = seq // tq
    nt_dims = (((1,), (1,)), ((), ()))

    vext_ref[:V_DIM, :] = v_ref[...].astype(F32).T.astype(BF16)
    vext_ref[V_DIM:, :] = jnp.ones((ATTN_ONES_ROWS, seq), BF16)

    lp = lam_ref[...]
    lam = (jnp.exp(jnp.sum(lp[0:1] * lp[1:2], axis=-1, keepdims=True))
           - jnp.exp(jnp.sum(lp[2:3] * lp[3:4], axis=-1, keepdims=True)) + LAM_INIT)
    lane = lax.broadcasted_iota(jnp.int32, (tq, LANES), 1)

    bufs = ((s0_ref, m0_ref, a0_ref), (s1_ref, m1_ref, a1_ref))

    def step(qk=None, pv=None, fin=None):
        if fin is not None:
            j, (_, _, fa_ref) = fin
            acc = fa_ref[...]
            o = acc[:V_DIM] / acc[V_DIM:V_DIM + 1]
            o = (o[:, :tq] - lam * o[:, tq:]).T
            o = _rms(o) * g_ref[...] * (1.0 - LAM_INIT)
            o_ref[pl.ds(pl.multiple_of(j * tq, tq), tq), :] = o.astype(BF16)
        if qk is not None:
            i, (qs_ref, qm_ref, _) = qk
            q = q_ref[pl.ds(pl.multiple_of(i * tq, tq), tq), :]
            zero = jnp.zeros_like(q)
            q2 = jnp.concatenate([jnp.where(lane < QK_DIM, q, zero),
                                  jnp.where(lane >= QK_DIM, q, zero)], axis=0)
            m = None
        if pv is not None:
            _, (ps_ref, pm_ref, pa_ref) = pv
            m_prev = pm_ref[...]
            acc = None
        for kb in range(seq // ATTN_KB):
            rows = slice(kb * ATTN_KB, (kb + 1) * ATTN_KB)
            if qk is not None:
                s = lax.dot_general(k_ref[rows, :], q2, nt_dims, preferred_element_type=F32)
                qs_ref[rows, :] = s
                blk = jnp.max(s, axis=0, keepdims=True)
                m = blk if m is None else jnp.maximum(m, blk)
            if pv is not None:
                for sub in range(ATTN_KB // ATTN_PB):
                    r0 = kb * ATTN_KB + sub * ATTN_PB
                    rows2 = slice(r0, r0 + ATTN_PB)
                    p = jnp.exp2(ps_ref[rows2, :] - m_prev).astype(BF16)
                    part = jnp.dot(vext_ref[:, rows2], p, preferred_element_type=F32)
                    acc = part if acc is None else acc + part
        if qk is not None:
            qm_ref[...] = m
        if pv is not None:
            pa_ref[...] = acc

    def unit(t, parity):
        return t, bufs[parity]

    step(qk=unit(0, 0))
    step(qk=unit(1, 1), pv=unit(0, 0))

    def body(jj, carry):
        t = 2 * jj
        step(qk=unit(t, 0), pv=unit(t - 1, 1), fin=unit(t - 2, 0))
        step(qk=unit(t + 1, 1), pv=unit(t, 0), fin=unit(t - 1, 1))
        return carry

    lax.fori_loop(1, nq // 2, body, 0)

    step(pv=unit(nq - 1, 1), fin=unit(nq - 2, 0))
    step(fin=unit(nq - 1, 1))


def _attn(lam_params, g_sub, dq, dk, dv):
    batch, heads, seq, _ = dq.shape
    tq = ATTN_TQ
    assert (seq // tq) % 2 == 0
    seq_spec = pl.BlockSpec((None, None, seq, LANES), lambda b, h: (b, h, 0, 0))
    return pl.pallas_call(
        _attn_kernel,
        grid=(batch, heads),
        in_specs=[_const_spec((4, QK_DIM)), _const_spec((1, V_DIM)), seq_spec, seq_spec, seq_spec],
        out_specs=pl.BlockSpec((seq, V_DIM), lambda b, h: (b, h)),
        out_shape=jax.ShapeDtypeStruct((batch * seq, HEADS * V_DIM), BF16),
        scratch_shapes=[
            pltpu.VMEM((V_DIM + ATTN_ONES_ROWS, seq), BF16),
            pltpu.VMEM((seq, 2 * tq), F32), pltpu.VMEM((seq, 2 * tq), F32),
            pltpu.VMEM((1, 2 * tq), F32), pltpu.VMEM((1, 2 * tq), F32),
            pltpu.VMEM((V_DIM + ATTN_ONES_ROWS, 2 * tq), F32),
            pltpu.VMEM((V_DIM + ATTN_ONES_ROWS, 2 * tq), F32),
        ],
        compiler_params=pltpu.CompilerParams(
            dimension_semantics=("parallel", "parallel"), vmem_limit_bytes=VMEM_LIMIT),
        name="diff_attn",
    )(lam_params, g_sub, dq, dk, dv)


def _log_sigmoid(x):
    return jnp.minimum(x, 0.0) - jnp.log1p(jnp.exp(-jnp.abs(x)))


def _ret_kernel(dec_ref, q_ref, k_ref, v_ref, rg_ref, o_ref, dmat_ref, u_ref, sf_ref, sb_ref):
    seq = q_ref.shape[0]
    c_len = RET_CHUNK
    n_chunks = seq // c_len
    pair_w = 2 * V_DIM
    nt_dims = (((1,), (1,)), ((), ()))
    tn_dims = (((0,), (0,)), ((), ()))

    lg = _log_sigmoid(dec_ref[...])
    row_i = lax.broadcasted_iota(jnp.int32, (c_len, c_len), 0)
    col_i = lax.broadcasted_iota(jnp.int32, (c_len, c_len), 1)
    diff = (row_i - col_i).astype(F32)
    for hh in range(2):
        dmat_ref[hh * c_len:(hh + 1) * c_len, :] = jnp.exp(
            jnp.where(diff >= 0, diff * lg[4 + 2 * hh:5 + 2 * hh], -diff * lg[5 + 2 * hh:6 + 2 * hh]))
    rows_k = lax.broadcasted_iota(jnp.int32, (c_len, LANES), 0).astype(F32)
    rows_v = lax.broadcasted_iota(jnp.int32, (c_len, pair_w), 0).astype(F32)
    zeta_f = jnp.exp((c_len - 1 - rows_k) * lg[0:1, :LANES])
    zeta_b = jnp.exp(rows_k * lg[1:2, :LANES])
    xi_f = jnp.exp((rows_v + 1) * lg[2:3])
    xi_b = jnp.exp((c_len - rows_v) * lg[3:4])
    srow = lax.broadcasted_iota(jnp.int32, (LANES, pair_w), 0)
    scol = lax.broadcasted_iota(jnp.int32, (LANES, pair_w), 1)
    head_a_row = srow < QK_DIM
    same_head = head_a_row == (scol < V_DIM)
    g_f = jnp.exp(c_len * jnp.where(head_a_row, lg[4:5], lg[6:7]))
    g_b = jnp.exp(c_len * jnp.where(head_a_row, lg[5:6], lg[7:8]))
    lane = lax.broadcasted_iota(jnp.int32, (c_len, LANES), 1)

    def chunk(c):
        return pl.ds(pl.multiple_of(c * c_len, c_len), c_len)

    def phase_a(c, carry):
        kc = k_ref[chunk(c), :].astype(F32)
        kz = jnp.concatenate([kc * zeta_f, kc * zeta_b], axis=1).astype(BF16)
        u_ref[c] = lax.dot_general(kz, v_ref[chunk(c), :], tn_dims, preferred_element_type=F32)
        return carry

    lax.fori_loop(0, n_chunks, phase_a, 0, unroll=8)

    def phase_b(t, carry):
        s_f, s_b = carry
        sf_ref[t] = jnp.where(same_head, s_f, 0.0).astype(BF16)
        s_f = g_f * s_f + u_ref[t, :LANES, :]
        cb = n_chunks - 1 - t
        sb_ref[cb] = jnp.where(same_head, s_b, 0.0).astype(BF16)
        s_b = g_b * s_b + u_ref[cb, LANES:, :]
        return s_f, s_b

    zero_state = jnp.zeros((LANES, pair_w), F32)
    lax.fori_loop(0, n_chunks, phase_b, (zero_state, zero_state))

    def phase_c(c, carry):
        q = q_ref[chunk(c), :]
        zero = jnp.zeros_like(q)
        q2 = jnp.concatenate([jnp.where(lane < QK_DIM, q, zero),
                              jnp.where(lane >= QK_DIM, q, zero)], axis=0)
        vc = v_ref[chunk(c), :]
        sc = lax.dot_general(q2, k_ref[chunk(c), :], nt_dims, preferred_element_type=F32) * dmat_ref[...]
        inner = jnp.dot(sc.astype(BF16), vc, preferred_element_type=F32)
        tot = (jnp.concatenate([inner[:c_len, :V_DIM], inner[c_len:, V_DIM:]], axis=1)
               + jnp.dot(q, sf_ref[c], preferred_element_type=F32) * xi_f
               + jnp.dot(q, sb_ref[c], preferred_element_type=F32) * xi_b)
        r = jnp.concatenate([_rms(tot[:, :V_DIM]), _rms(tot[:, V_DIM:])], axis=1)
        gate = rg_ref[chunk(c), :].astype(F32)
        o_ref[chunk(c), :] = (r * gate * jax.nn.sigmoid(gate)).astype(BF16)
        return carry

    lax.fori_loop(0, n_chunks, phase_c, 0, unroll=8)


def _ret(dec, rq, rk, rv, rg):
    batch, pairs, seq, _ = rq.shape
    pair_w = 2 * V_DIM
    n_chunks = seq // RET_CHUNK
    assert RET_CHUNK == pair_w
    return pl.pallas_call(
        _ret_kernel,
        grid=(batch, pairs),
        in_specs=[
            pl.BlockSpec((None, 8, pair_w), lambda b, p: (p, 0, 0)),
            pl.BlockSpec((None, None, seq, LANES), lambda b, p: (b, p, 0, 0)),
            pl.BlockSpec((None, None, seq, LANES), lambda b, p: (b, p, 0, 0)),
            pl.BlockSpec((None, None, seq, pair_w), lambda b, p: (b, p, 0, 0)),
            pl.BlockSpec((None, None, seq, pair_w), lambda b, p: (b, p, 0, 0)),
        ],
        out_specs=pl.BlockSpec((seq, pair_w), lambda b, p: (b, p)),
        out_shape=jax.ShapeDtypeStruct((batch * seq, RET_V_W), BF16),
        scratch_shapes=[
            pltpu.VMEM((2 * RET_CHUNK, RET_CHUNK), F32),
            pltpu.VMEM((n_chunks, 2 * LANES, pair_w), F32),
            pltpu.VMEM((n_chunks, LANES, pair_w), BF16),
            pltpu.VMEM((n_chunks, LANES, pair_w), BF16),
        ],
        compiler_params=pltpu.CompilerParams(
            dimension_semantics=("parallel", "parallel"), vmem_limit_bytes=VMEM_LIMIT),
        name="retention",
    )(dec, rq, rk, rv, rg)


def _post_kernel(x_ref, a_ref, r_ref, ga_ref, gb_ref, wa_ref, wr_ref, wo_ref,
                 g2_ref, gf_ref, wg_ref, wu_ref, wd_ref, o_ref):
    ya = jnp.dot(a_ref[...], wa_ref[...], preferred_element_type=F32)
    yb = jnp.dot(r_ref[...], wr_ref[...], preferred_element_type=F32)
    m = (jax.nn.sigmoid(ga_ref[...].astype(F32)) * ya
         + jax.nn.sigmoid(gb_ref[...].astype(F32)) * yb)
    x = x_ref[...] + jnp.dot(m.astype(BF16), wo_ref[...], preferred_element_type=F32)
    h = (_rms(x) * g2_ref[...]).astype(BF16)
    gate = jnp.dot(h, wg_ref[...], preferred_element_type=F32)
    up = jnp.dot(h, wu_ref[...], preferred_element_type=F32)
    act = (gate * jax.nn.sigmoid(gate) * up).astype(BF16)
    x = x + jnp.dot(act, wd_ref[...], preferred_element_type=F32)
    o_ref[...] = _rms(x) * gf_ref[...]


def _post(x2d, a, r, ga, gb, wa, wr, wo, g_ffn, g_final, wg, wu, wd):
    tokens = x2d.shape[0]
    tm = POST_TM
    row = lambda w: pl.BlockSpec((tm, w), lambda i: (i, 0))
    consts = [wa, wr, wo, g_ffn, g_final, wg, wu, wd]
    return pl.pallas_call(
        _post_kernel,
        grid=(tokens // tm,),
        in_specs=[row(D_MODEL), row(DIFF_W), row(RET_V_W), row(D_MODEL), row(D_MODEL)]
        + [_const_spec(c.shape) for c in consts],
        out_specs=row(D_MODEL),
        out_shape=jax.ShapeDtypeStruct((tokens, D_MODEL), F32),
        compiler_params=pltpu.CompilerParams(
            dimension_semantics=("parallel",), vmem_limit_bytes=VMEM_LIMIT),
        name="post",
    )(x2d, a, r, ga, gb, *consts)


def _rope_tables(seq):
    half = QK_DIM // 2
    inv_freq = ROPE_THETA ** (-jnp.arange(0, QK_DIM, 2, dtype=F32) / QK_DIM)
    ang = jnp.arange(seq, dtype=F32)[:, None] * inv_freq[None, :]
    cos, sin = jnp.cos(ang), jnp.sin(ang)
    cos_t = jnp.tile(cos, (1, LANES // half))
    sin_t = jnp.tile(jnp.concatenate([-sin, sin], axis=-1), (1, LANES // QK_DIM))
    return cos_t, sin_t


def _decay_table(fwd, bwd):
    pairs = HEADS // 2
    f = fwd.astype(F32).reshape(pairs, 2)
    b = bwd.astype(F32).reshape(pairs, 2)
    k_lanes = lambda t: jnp.tile(jnp.repeat(t, QK_DIM, axis=1), (1, 2))
    v_lanes = lambda t: jnp.repeat(t, V_DIM, axis=1)
    all_lanes = lambda col: jnp.broadcast_to(col[:, None], (pairs, 2 * V_DIM))
    rows = [k_lanes(f), k_lanes(b), v_lanes(f), v_lanes(b),
            all_lanes(f[:, 0]), all_lanes(b[:, 0]), all_lanes(f[:, 1]), all_lanes(b[:, 1])]
    return jnp.stack(rows, axis=1)


def kernel(x, g_mix, w_in, diff_lq1, diff_lk1, diff_lq2, diff_lk2, diff_subln_g,
           ret_decay_fwd, ret_decay_bwd, w_up_diff, w_up_ret, w_o,
           g_ffn, w_ffn_gate, w_ffn_up, w_ffn_down, g_final):
    batch, seq, _ = x.shape
    layer = 0
    x2d = x.reshape(batch * seq, D_MODEL)
    cos_t, sin_t = _rope_tables(seq)

    later = [w_up_diff[layer], w_up_ret[layer], w_o[layer], w_ffn_gate[layer], w_ffn_up[layer],
             w_ffn_down[layer]]
    (dq, dk, dv, rq, rk, rv, rg, ga, gb, wa, wr, wo, wg, wu, wd) = _proj(
        x2d, g_mix[layer][None, :], w_in[layer], cos_t, sin_t, later, batch, seq)

    lam_params = jnp.stack([diff_lq1[layer], diff_lk1[layer], diff_lq2[layer], diff_lk2[layer]])
    a = _attn(lam_params.astype(F32), diff_subln_g[layer][None, :].astype(F32), dq, dk, dv)

    r = _ret(_decay_table(ret_decay_fwd[layer], ret_decay_bwd[layer]), rq, rk, rv, rg)

    y = _post(x2d, a, r, ga, gb, wa, wr, wo, g_ffn[layer][None, :], g_final[None, :], wg, wu, wd)
    return y.reshape(batch, seq, D_MODEL)
```

```python
import math

import jax
import jax.numpy as jnp
from jax import lax
from jax.experimental import pallas as pl
from jax.experimental.pallas import tpu as pltpu

F32 = jnp.float32
BF16 = jnp.bfloat16

D_MODEL = 1024
HEADS = 4
QK_DIM = 64
V_DIM = 128
DIFF_W = HEADS * 2 * QK_DIM
RET_QK_W = HEADS * QK_DIM
RET_V_W = HEADS * V_DIM
IN_COLS = 3 * DIFF_W + 2 * RET_QK_W + 2 * RET_V_W + 2 * D_MODEL
D_FF = 2816
ROPE_THETA = 10000.0
NORM_EPS = 1e-5
LAM_INIT = 0.8 - 0.6 * math.exp(-0.3 * 0)
LOG2E = 1.4426950408889634

LANES = 128
VMEM_LIMIT = 56 * 1024 * 1024
PROJ_TM = 512
PROJ_CAST_COLS = 512
N_LATER_WEIGHTS = 6
BF16_SUBLANES = 16
ATTN_TQ = 256
ATTN_ONES_ROWS = 16
ATTN_KB = 512
ATTN_PB = 256
RET_CHUNK = 256
POST_TM = 512


def _rms(x, eps=NORM_EPS):
    return x * lax.rsqrt(jnp.mean(x * x, axis=-1, keepdims=True) + eps)


def _const_spec(shape):
    nd = len(shape)
    return pl.BlockSpec(shape, lambda *_: (0,) * nd, pipeline_mode=pl.Buffered(1))


def _proj_kernel(x_ref, g_ref, wf_ref, cos_ref, sin_ref, *refs):
    later_f32 = refs[:N_LATER_WEIGHTS]
    dq_ref, dk_ref, dv_ref, rq_ref, rk_ref, rv_ref, rg_ref, ga_ref, gb_ref = refs[N_LATER_WEIGHTS:-N_LATER_WEIGHTS - 1]
    later_bf16 = refs[-N_LATER_WEIGHTS - 1:-1]
    wb_ref = refs[-1]
    tm = x_ref.shape[0]

    @pl.when(pl.program_id(0) == 0)
    def _():
        for c in range(0, IN_COLS, PROJ_CAST_COLS):
            wb_ref[:, c:c + PROJ_CAST_COLS] = wf_ref[:, c:c + PROJ_CAST_COLS].astype(BF16)

    for src_ref, dst_ref in zip(later_f32, later_bf16):
        dst_ref[...] = src_ref[...].astype(BF16)

    hb = (_rms(x_ref[...]) * g_ref[...]).astype(BF16)
    cos = cos_ref[...]
    sin = sin_ref[...]
    lane = lax.broadcasted_iota(jnp.int32, (tm, LANES), 1)
    first_half = (lane % QK_DIM) < (QK_DIM // 2)

    def proj(c0, n):
        return jnp.dot(hb, wb_ref[:, c0:c0 + n], preferred_element_type=F32)

    def rope(y, scale):
        partner = jnp.where(first_half, pltpu.roll(y, LANES - 32, 1), pltpu.roll(y, 32, 1))
        return ((y * cos + partner * sin) * scale).astype(BF16)

    q_scale = QK_DIM ** -0.5 * LOG2E
    y = proj(0, DIFF_W)
    for h in range(HEADS):
        dq_ref[h] = rope(y[:, h * LANES:(h + 1) * LANES], q_scale)
    y = proj(DIFF_W, DIFF_W)
    for h in range(HEADS):
        dk_ref[h] = rope(y[:, h * LANES:(h + 1) * LANES], 1.0)
    y = proj(2 * DIFF_W, DIFF_W)
    for h in range(HEADS):
        dv_ref[h] = y[:, h * LANES:(h + 1) * LANES].astype(BF16)
    c0 = 3 * DIFF_W
    y = proj(c0, 2 * RET_QK_W)
    for p in range(HEADS // 2):
        rq_ref[p] = rope(y[:, p * LANES:(p + 1) * LANES], 1.0)
        rk_ref[p] = rope(y[:, RET_QK_W + p * LANES:RET_QK_W + (p + 1) * LANES], QK_DIM ** -0.5)
    c0 += 2 * RET_QK_W
    y = proj(c0, RET_V_W)
    for p in range(HEADS // 2):
        rv_ref[p] = y[:, p * 2 * V_DIM:(p + 1) * 2 * V_DIM].astype(BF16)
    c0 += RET_V_W
    y = proj(c0, RET_V_W)
    for p in range(HEADS // 2):
        rg_ref[p] = y[:, p * 2 * V_DIM:(p + 1) * 2 * V_DIM].astype(BF16)
    c0 += RET_V_W
    for half in range(2):
        ga_ref[:, half * 512:(half + 1) * 512] = proj(c0 + half * 512, 512).astype(BF16)
    c0 += D_MODEL
    for half in range(2):
        gb_ref[:, half * 512:(half + 1) * 512] = proj(c0 + half * 512, 512).astype(BF16)


def _proj(x2d, g_mix, w_in, cos_t, sin_t, later_weights, batch, seq):
    tm = PROJ_TM
    nb = seq // tm
    tokens = batch * seq
    steps = tokens // tm
    assert len(later_weights) == N_LATER_WEIGHTS
    slab_specs = []
    for w in later_weights:
        n_slabs = max(n for n in range(1, steps + 1)
                      if w.shape[0] % n == 0 and (w.shape[0] // n) % BF16_SUBLANES == 0)
        slab_specs.append(pl.BlockSpec((w.shape[0] // n_slabs, w.shape[1]),
                                       lambda i, last=n_slabs - 1: (jnp.minimum(i, last), 0)))

    def head_major(n_groups, width):
        shape = jax.ShapeDtypeStruct((batch, n_groups, seq, width), BF16)
        spec = pl.BlockSpec((None, n_groups, tm, width), lambda i: (i // nb, 0, i % nb, 0))
        return shape, spec

    outs = [head_major(HEADS, LANES)] * 3 + [head_major(HEADS // 2, LANES)] * 2 \
        + [head_major(HEADS // 2, 2 * V_DIM)] * 2
    out_shape = [o[0] for o in outs] + [jax.ShapeDtypeStruct((tokens, D_MODEL), BF16)] * 2 \
        + [jax.ShapeDtypeStruct(w.shape, BF16) for w in later_weights]
    out_specs = [o[1] for o in outs] + [pl.BlockSpec((tm, D_MODEL), lambda i: (i, 0))] * 2 + slab_specs
    return pl.pallas_call(
        _proj_kernel,
        grid=(steps,),
        in_specs=[
            pl.BlockSpec((tm, D_MODEL), lambda i: (i, 0)),
            _const_spec((1, D_MODEL)),
            _const_spec((D_MODEL, IN_COLS)),
            pl.BlockSpec((tm, LANES), lambda i: (i % nb, 0)),
            pl.BlockSpec((tm, LANES), lambda i: (i % nb, 0)),
        ] + slab_specs,
        out_specs=out_specs,
        out_shape=out_shape,
        scratch_shapes=[pltpu.VMEM((D_MODEL, IN_COLS), BF16)],
        compiler_params=pltpu.CompilerParams(
            dimension_semantics=("arbitrary",), vmem_limit_bytes=VMEM_LIMIT),
        name="in_proj",
    )(x2d, g_mix, w_in, cos_t, sin_t, *later_weights)


def _attn_kernel(lam_ref, g_ref, q_ref, k_ref, v_ref, o_ref,
                 vext_ref, s0_ref, s1_ref, m0_ref, m1_ref, a0_ref, a1_ref):
    seq = q_ref.shape[0]
    tq = ATTN_TQ
    nq = seq // tq
    nt_dims = (((1,), (1,)), ((), ()))

    vext_ref[:V_DIM, :] = v_ref[...].astype(F32).T.astype(BF16)
    vext_ref[V_DIM:, :] = jnp.ones((ATTN_ONES_ROWS, seq), BF16)

    lp = lam_ref[...]
    lam = (jnp.exp(jnp.sum(lp[0:1] * lp[1:2], axis=-1, keepdims=True))
           - jnp.exp(jnp.sum(lp[2:3] * lp[3:4], axis=-1, keepdims=True)) + LAM_INIT)
    lane = lax.broadcasted_iota(jnp.int32, (tq, LANES), 1)

    bufs = ((s0_ref, m0_ref, a0_ref), (s1_ref, m1_ref, a1_ref))

    def step(qk=None, pv=None, fin=None):
        if fin is not None:
            j, (_, _, fa_ref) = fin
            acc = fa_ref[...]
            o = acc[:V_DIM] / acc[V_DIM:V_DIM + 1]
            o = (o[:, :tq] - lam * o[:, tq:]).T
            o = _rms(o) * g_ref[...] * (1.0 - LAM_INIT)
            o_ref[pl.ds(pl.multiple_of(j * tq, tq), tq), :] = o.astype(BF16)
        if qk is not None:
            i, (qs_ref, qm_ref, _) = qk
            q = q_ref[pl.ds(pl.multiple_of(i * tq, tq), tq), :]
            zero = jnp.zeros_like(q)
            q2 = jnp.concatenate([jnp.where(lane < QK_DIM, q, zero),
                                  jnp.where(lane >= QK_DIM, q, zero)], axis=0)
            m = None
        if pv is not None:
            _, (ps_ref, pm_ref, pa_ref) = pv
            m_prev = pm_ref[...]
            acc = None
        for kb in range(seq // ATTN_KB):
            rows = slice(kb * ATTN_KB, (kb + 1) * ATTN_KB)
            if qk is not None:
                s = lax.dot_general(k_ref[rows, :], q2, nt_dims, preferred_element_type=F32)
                qs_ref[rows, :] = s
                blk = jnp.max(s, axis=0, keepdims=True)
                m = blk if m is None else jnp.maximum(m, blk)
            if pv is not None:
                for sub in range(ATTN_KB // ATTN_PB):
                    r0 = kb * ATTN_KB + sub * ATTN_PB
                    rows2 = slice(r0, r0 + ATTN_PB)
                    p = jnp.exp2(ps_ref[rows2, :] - m_prev).astype(BF16)
                    part = jnp.dot(vext_ref[:, rows2], p, preferred_element_type=F32)
                    acc = part if acc is None else acc + part
        if qk is not None:
            qm_ref[...] = m
        if pv is not None:
            pa_ref[...] = acc

    def unit(t, parity):
        return t, bufs[parity]

    step(qk=unit(0, 0))
    step(qk=unit(1, 1), pv=unit(0, 0))

    def body(jj, carry):
        t = 2 * jj
        step(qk=unit(t, 0), pv=unit(t - 1, 1), fin=unit(t - 2, 0))
        step(qk=unit(t + 1, 1), pv=unit(t, 0), fin=unit(t - 1, 1))
        return carry

    lax.fori_loop(1, nq // 2, body, 0)

    step(pv=unit(nq - 1, 1), fin=unit(nq - 2, 0))
    step(fin=unit(nq - 1, 1))


def _attn(lam_params, g_sub, dq, dk, dv):
    batch, heads, seq, _ = dq.shape
    tq = ATTN_TQ
    assert (seq // tq) % 2 == 0
    seq_spec = pl.BlockSpec((None, None, seq, LANES), lambda b, h: (b, h, 0, 0))
    return pl.pallas_call(
        _attn_kernel,
        grid=(batch, heads),
        in_specs=[_const_spec((4, QK_DIM)), _const_spec((1, V_DIM)), seq_spec, seq_spec, seq_spec],
        out_specs=pl.BlockSpec((seq, V_DIM), lambda b, h: (b, h)),
        out_shape=jax.ShapeDtypeStruct((batch * seq, HEADS * V_DIM), BF16),
        scratch_shapes=[
            pltpu.VMEM((V_DIM + ATTN_ONES_ROWS, seq), BF16),
            pltpu.VMEM((seq, 2 * tq), F32), pltpu.VMEM((seq, 2 * tq), F32),
            pltpu.VMEM((1, 2 * tq), F32), pltpu.VMEM((1, 2 * tq), F32),
            pltpu.VMEM((V_DIM + ATTN_ONES_ROWS, 2 * tq), F32),
            pltpu.VMEM((V_DIM + ATTN_ONES_ROWS, 2 * tq), F32),
        ],
        compiler_params=pltpu.CompilerParams(
            dimension_semantics=("parallel", "parallel"), vmem_limit_bytes=VMEM_LIMIT),
        name="diff_attn",
    )(lam_params, g_sub, dq, dk, dv)


def _log_sigmoid(x):
    return jnp.minimum(x, 0.0) - jnp.log1p(jnp.exp(-jnp.abs(x)))


def _ret_kernel(dec_ref, q_ref, k_ref, v_ref, rg_ref, o_ref, dmat_ref, u_ref, sf_ref, sb_ref):
    seq = q_ref.shape[0]
    c_len = RET_CHUNK
    n_chunks = seq // c_len
    pair_w = 2 * V_DIM
    nt_dims = (((1,), (1,)), ((), ()))
    tn_dims = (((0,), (0,)), ((), ()))

    lg = _log_sigmoid(dec_ref[...])
    row_i = lax.broadcasted_iota(jnp.int32, (c_len, c_len), 0)
    col_i = lax.broadcasted_iota(jnp.int32, (c_len, c_len), 1)
    diff = (row_i - col_i).astype(F32)
    for hh in range(2):
        dmat_ref[hh * c_len:(hh + 1) * c_len, :] = jnp.exp(
            jnp.where(diff >= 0, diff * lg[4 + 2 * hh:5 + 2 * hh], -diff * lg[5 + 2 * hh:6 + 2 * hh]))
    rows_k = lax.broadcasted_iota(jnp.int32, (c_len, LANES), 0).astype(F32)
    rows_v = lax.broadcasted_iota(jnp.int32, (c_len, pair_w), 0).astype(F32)
    zeta_f = jnp.exp((c_len - 1 - rows_k) * lg[0:1, :LANES])
    zeta_b = jnp.exp(rows_k * lg[1:2, :LANES])
    xi_f = jnp.exp((rows_v + 1) * lg[2:3])
    xi_b = jnp.exp((c_len - rows_v) * lg[3:4])
    srow = lax.broadcasted_iota(jnp.int32, (LANES, pair_w), 0)
    scol = lax.broadcasted_iota(jnp.int32, (LANES, pair_w), 1)
    head_a_row = srow < QK_DIM
    same_head = head_a_row == (scol < V_DIM)
    g_f = jnp.exp(c_len * jnp.where(head_a_row, lg[4:5], lg[6:7]))
    g_b = jnp.exp(c_len * jnp.where(head_a_row, lg[5:6], lg[7:8]))
    lane = lax.broadcasted_iota(jnp.int32, (c_len, LANES), 1)

    def chunk(c):
        return pl.ds(pl.multiple_of(c * c_len, c_len), c_len)

    def phase_a(c, carry):
        kc = k_ref[chunk(c), :].astype(F32)
        kz = jnp.concatenate([kc * zeta_f, kc * zeta_b], axis=1).astype(BF16)
        u_ref[c] = lax.dot_general(kz, v_ref[chunk(c), :], tn_dims, preferred_element_type=F32)
        return carry

    lax.fori_loop(0, n_chunks, phase_a, 0, unroll=16)

    def phase_b(t, carry):
        s_f, s_b = carry
        sf_ref[t] = jnp.where(same_head, s_f, 0.0).astype(BF16)
        s_f = g_f * s_f + u_ref[t, :LANES, :]
        cb = n_chunks - 1 - t
        sb_ref[cb] = jnp.where(same_head, s_b, 0.0).astype(BF16)
        s_b = g_b * s_b + u_ref[cb, LANES:, :]
        return s_f, s_b

    zero_state = jnp.zeros((LANES, pair_w), F32)
    lax.fori_loop(0, n_chunks, phase_b, (zero_state, zero_state))

    def phase_c(c, carry):
        q = q_ref[chunk(c), :]
        zero = jnp.zeros_like(q)
        q2 = jnp.concatenate([jnp.where(lane < QK_DIM, q, zero),
                              jnp.where(lane >= QK_DIM, q, zero)], axis=0)
        vc = v_ref[chunk(c), :]
        sc = lax.dot_general(q2, k_ref[chunk(c), :], nt_dims, preferred_element_type=F32) * dmat_ref[...]
        inner = jnp.dot(sc.astype(BF16), vc, preferred_element_type=F32)
        tot = (jnp.concatenate([inner[:c_len, :V_DIM], inner[c_len:, V_DIM:]], axis=1)
               + jnp.dot(q, sf_ref[c], preferred_element_type=F32) * xi_f
               + jnp.dot(q, sb_ref[c], preferred_element_type=F32) * xi_b)
        r = jnp.concatenate([_rms(tot[:, :V_DIM]), _rms(tot[:, V_DIM:])], axis=1)
        gate = rg_ref[chunk(c), :].astype(F32)
        o_ref[chunk(c), :] = (r * gate * jax.nn.sigmoid(gate)).astype(BF16)
        return carry

    lax.fori_loop(0, n_chunks, phase_c, 0, unroll=16)


def _ret(dec, rq, rk, rv, rg):
    batch, pairs, seq, _ = rq.shape
    pair_w = 2 * V_DIM
    n_chunks = seq // RET_CHUNK
    assert RET_CHUNK == pair_w
    return pl.pallas_call(
        _ret_kernel,
        grid=(batch, pairs),
        in_specs=[
            pl.BlockSpec((None, 8, pair_w), lambda b, p: (p, 0, 0)),
            pl.BlockSpec((None, None, seq, LANES), lambda b, p: (b, p, 0, 0)),
            pl.BlockSpec((None, None, seq, LANES), lambda b, p: (b, p, 0, 0)),
            pl.BlockSpec((None, None, seq, pair_w), lambda b, p: (b, p, 0, 0)),
            pl.BlockSpec((None, None, seq, pair_w), lambda b, p: (b, p, 0, 0)),
        ],
        out_specs=pl.BlockSpec((seq, pair_w), lambda b, p: (b, p)),
        out_shape=jax.ShapeDtypeStruct((batch * seq, RET_V_W), BF16),
        scratch_shapes=[
            pltpu.VMEM((2 * RET_CHUNK, RET_CHUNK), F32),
            pltpu.VMEM((n_chunks, 2 * LANES, pair_w), F32),
            pltpu.VMEM((n_chunks, LANES, pair_w), BF16),
            pltpu.VMEM((n_chunks, LANES, pair_w), BF16),
        ],
        compiler_params=pltpu.CompilerParams(
            dimension_semantics=("parallel", "parallel"), vmem_limit_bytes=VMEM_LIMIT),
        name="retention",
    )(dec, rq, rk, rv, rg)


def _post_kernel(x_ref, a_ref, r_ref, ga_ref, gb_ref, wa_ref, wr_ref, wo_ref,
                 g2_ref, gf_ref, wg_ref, wu_ref, wd_ref, o_ref):
    ya = jnp.dot(a_ref[...], wa_ref[...], preferred_element_type=F32)
    yb = jnp.dot(r_ref[...], wr_ref[...], preferred_element_type=F32)
    m = (jax.nn.sigmoid(ga_ref[...].astype(F32)) * ya
         + jax.nn.sigmoid(gb_ref[...].astype(F32)) * yb)
    x = x_ref[...] + jnp.dot(m.astype(BF16), wo_ref[...], preferred_element_type=F32)
    h = (_rms(x) * g2_ref[...]).astype(BF16)
    gate = jnp.dot(h, wg_ref[...], preferred_element_type=F32)
    up = jnp.dot(h, wu_ref[...], preferred_element_type=F32)
    act = (gate * jax.nn.sigmoid(gate) * up).astype(BF16)
    x = x + jnp.dot(act, wd_ref[...], preferred_element_type=F32)
    o_ref[...] = _rms(x) * gf_ref[...]


def _post(x2d, a, r, ga, gb, wa, wr, wo, g_ffn, g_final, wg, wu, wd):
    tokens = x2d.shape[0]
    tm = POST_TM
    row = lambda w: pl.BlockSpec((tm, w), lambda i: (i, 0))
    consts = [wa, wr, wo, g_ffn, g_final, wg, wu, wd]
    return pl.pallas_call(
        _post_kernel,
        grid=(tokens // tm,),
        in_specs=[row(D_MODEL), row(DIFF_W), row(RET_V_W), row(D_MODEL), row(D_MODEL)]
        + [_const_spec(c.shape) for c in consts],
        out_specs=row(D_MODEL),
        out_shape=jax.ShapeDtypeStruct((tokens, D_MODEL), F32),
        compiler_params=pltpu.CompilerParams(
            dimension_semantics=("parallel",), vmem_limit_bytes=VMEM_LIMIT),
        name="post",
    )(x2d, a, r, ga, gb, *consts)


def _rope_tables(seq):
    half = QK_DIM // 2
    inv_freq = ROPE_THETA ** (-jnp.arange(0, QK_DIM, 2, dtype=F32) / QK_DIM)
    inv_freq = jnp.tile(inv_freq, LANES // half)
    sign = jnp.tile(jnp.repeat(jnp.array([-1.0, 1.0], F32), half), LANES // QK_DIM)
    ang = jnp.arange(seq, dtype=F32)[:, None] * inv_freq[None, :]
    return jnp.cos(ang), jnp.sin(ang) * sign[None, :]


def _decay_table(fwd, bwd):
    pairs = HEADS // 2
    f = fwd.astype(F32).reshape(pairs, 2)
    b = bwd.astype(F32).reshape(pairs, 2)
    k_lanes = lambda t: jnp.tile(jnp.repeat(t, QK_DIM, axis=1), (1, 2))
    v_lanes = lambda t: jnp.repeat(t, V_DIM, axis=1)
    all_lanes = lambda col: jnp.broadcast_to(col[:, None], (pairs, 2 * V_DIM))
    rows = [k_lanes(f), k_lanes(b), v_lanes(f), v_lanes(b),
            all_lanes(f[:, 0]), all_lanes(b[:, 0]), all_lanes(f[:, 1]), all_lanes(b[:, 1])]
    return jnp.stack(rows, axis=1)


def kernel(x, g_mix, w_in, diff_lq1, diff_lk1, diff_lq2, diff_lk2, diff_subln_g,
           ret_decay_fwd, ret_decay_bwd, w_up_diff, w_up_ret, w_o,
           g_ffn, w_ffn_gate, w_ffn_up, w_ffn_down, g_final):
    batch, seq, _ = x.shape
    assert w_in.shape[0] == 1, "single-layer block: LAM_INIT and the call chain are for depth 1"
    layer = 0
    x2d =x.reshape(batch * seq, D_MODEL)
    cos_t, sin_t = _rope_tables(seq)

    later = [w_up_diff[layer], w_up_ret[layer], w_o[layer], w_ffn_gate[layer], w_ffn_up[layer],
             w_ffn_down[layer]]
    (dq, dk, dv, rq, rk, rv, rg, ga, gb, wa, wr, wo, wg, wu, wd) = _proj(
        x2d, g_mix[layer][None, :], w_in[layer], cos_t, sin_t, later, batch, seq)

    lam_params = jnp.stack([diff_lq1[layer], diff_lk1[layer], diff_lq2[layer], diff_lk2[layer]])
    a = _attn(lam_params.astype(F32), diff_subln_g[layer][None, :].astype(F32), dq, dk, dv)

    r = _ret(_decay_table(ret_decay_fwd[layer], ret_decay_bwd[layer]), rq, rk, rv, rg)

    y = _post(x2d, a, r, ga, gb, wa, wr, wo, g_ffn[layer][None, :], g_final[None, :], wg, wu, wd)
    return y.reshape(batch, seq, D_MODEL)
```

```python
import math

import jax
import jax.numpy as jnp
from jax import lax
from jax.experimental import pallas as pl
from jax.experimental.pallas import tpu as pltpu

F32 = jnp.float32
BF16 = jnp.bfloat16

D_MODEL = 1024
HEADS = 4
QK_DIM = 64
V_DIM = 128
DIFF_W = HEADS * 2 * QK_DIM
RET_QK_W = HEADS * QK_DIM
RET_V_W = HEADS * V_DIM
IN_COLS = 3 * DIFF_W + 2 * RET_QK_W + 2 * RET_V_W + 2 * D_MODEL
D_FF = 2816
ROPE_THETA = 10000.0
NORM_EPS = 1e-5
LAM_INIT = 0.8 - 0.6 * math.exp(-0.3 * 0)
LOG2E = 1.4426950408889634

LANES = 128
VMEM_LIMIT = 56 * 1024 * 1024
PROJ_TM = 512
PROJ_CAST_COLS = 512
N_LATER_WEIGHTS = 6
BF16_SUBLANES = 16
ATTN_TQ = 256
ATTN_ONES_ROWS = 16
ATTN_KB = 512
ATTN_PB = 256
RET_CHUNK = 256
POST_TM = 512


def _rms(x, eps=NORM_EPS):
    return x * lax.rsqrt(jnp.mean(x * x, axis=-1, keepdims=True) + eps)


def _const_spec(shape):
    nd = len(shape)
    return pl.BlockSpec(shape, lambda *_: (0,) * nd, pipeline_mode=pl.Buffered(1))


def _proj_kernel(x_ref, g_ref, wf_ref, cos_ref, sin_ref, *refs):
    later_f32 = refs[:N_LATER_WEIGHTS]
    dq_ref, dk_ref, dv_ref, rq_ref, rk_ref, rv_ref, rg_ref, ga_ref, gb_ref = refs[N_LATER_WEIGHTS:-N_LATER_WEIGHTS - 1]
    later_bf16 = refs[-N_LATER_WEIGHTS - 1:-1]
    wb_ref = refs[-1]
    tm = x_ref.shape[0]

    @pl.when(pl.program_id(0) == 0)
    def _():
        for c in range(0, IN_COLS, PROJ_CAST_COLS):
            wb_ref[:, c:c + PROJ_CAST_COLS] = wf_ref[:, c:c + PROJ_CAST_COLS].astype(BF16)

    for src_ref, dst_ref in zip(later_f32, later_bf16):
        dst_ref[...] = src_ref[...].astype(BF16)

    hb = (_rms(x_ref[...]) * g_ref[...]).astype(BF16)
    cos = cos_ref[...]
    sin = sin_ref[...]
    lane = lax.broadcasted_iota(jnp.int32, (tm, LANES), 1)
    first_half = (lane % QK_DIM) < (QK_DIM // 2)

    def proj(c0, n):
        return jnp.dot(hb, wb_ref[:, c0:c0 + n], preferred_element_type=F32)

    def rope(y, scale):
        partner = jnp.where(first_half, pltpu.roll(y, LANES - 32, 1), pltpu.roll(y, 32, 1))
        return ((y * cos + partner * sin) * scale).astype(BF16)

    q_scale = QK_DIM ** -0.5 * LOG2E
    y = proj(0, DIFF_W)
    for h in range(HEADS):
        dq_ref[h] = rope(y[:, h * LANES:(h + 1) * LANES], q_scale)
    y = proj(DIFF_W, DIFF_W)
    for h in range(HEADS):
        dk_ref[h] = rope(y[:, h * LANES:(h + 1) * LANES], 1.0)
    y = proj(2 * DIFF_W, DIFF_W)
    for h in range(HEADS):
        dv_ref[h] = y[:, h * LANES:(h + 1) * LANES].astype(BF16)
    c0 = 3 * DIFF_W
    y = proj(c0, 2 * RET_QK_W)
    for p in range(HEADS // 2):
        rq_ref[p] = rope(y[:, p * LANES:(p + 1) * LANES], 1.0)
        rk_ref[p] = rope(y[:, RET_QK_W + p * LANES:RET_QK_W + (p + 1) * LANES], QK_DIM ** -0.5)
    c0 += 2 * RET_QK_W
    y = proj(c0, RET_V_W)
    for p in range(HEADS // 2):
        rv_ref[p] = y[:, p * 2 * V_DIM:(p + 1) * 2 * V_DIM].astype(BF16)
    c0 += RET_V_W
    y = proj(c0, RET_V_W)
    for p in range(HEADS // 2):
        rg_ref[p] = y[:, p * 2 * V_DIM:(p + 1) * 2 * V_DIM].astype(BF16)
    c0 += RET_V_W
    for half in range(2):
        ga_ref[:, half * 512:(half + 1) * 512] = proj(c0 + half * 512, 512).astype(BF16)
    c0 += D_MODEL
    for half in range(2):
        gb_ref[:, half * 512:(half + 1) * 512] = proj(c0 + half * 512, 512).astype(BF16)


def _proj(x2d, g_mix, w_in, cos_t, sin_t, later_weights, batch, seq):
    tm = PROJ_TM
    nb = seq // tm
    tokens = batch * seq
    steps = tokens // tm
    assert len(later_weights) == N_LATER_WEIGHTS
    slab_specs = []
    for w in later_weights:
        n_slabs = max(n for n in range(1, steps + 1)
                      if w.shape[0] % n == 0 and (w.shape[0] // n) % BF16_SUBLANES == 0)
        slab_specs.append(pl.BlockSpec((w.shape[0] // n_slabs, w.shape[1]),
                                       lambda i, last=n_slabs - 1: (jnp.minimum(i, last), 0)))

    def head_major(n_groups, width):
        shape = jax.ShapeDtypeStruct((batch, n_groups, seq, width), BF16)
        spec = pl.BlockSpec((None, n_groups, tm, width), lambda i: (i // nb, 0, i % nb, 0))
        return shape, spec

    outs = [head_major(HEADS, LANES)] * 3 + [head_major(HEADS // 2, LANES)] * 2 \
        + [head_major(HEADS // 2, 2 * V_DIM)] * 2
    out_shape = [o[0] for o in outs] + [jax.ShapeDtypeStruct((tokens, D_MODEL), BF16)] * 2 \
        + [jax.ShapeDtypeStruct(w.shape, BF16) for w in later_weights]
    out_specs = [o[1] for o in outs] + [pl.BlockSpec((tm, D_MODEL), lambda i: (i, 0))] * 2 + slab_specs
    return pl.pallas_call(
        _proj_kernel,
        grid=(steps,),
        in_specs=[
            pl.BlockSpec((tm, D_MODEL), lambda i: (i, 0)),
            _const_spec((1, D_MODEL)),
            _const_spec((D_MODEL, IN_COLS)),
            pl.BlockSpec((tm, LANES), lambda i: (i % nb, 0)),
            pl.BlockSpec((tm, LANES), lambda i: (i % nb, 0)),
        ] + slab_specs,
        out_specs=out_specs,
        out_shape=out_shape,
        scratch_shapes=[pltpu.VMEM((D_MODEL, IN_COLS), BF16)],
        compiler_params=pltpu.CompilerParams(
            dimension_semantics=("arbitrary",), vmem_limit_bytes=VMEM_LIMIT),
        name="in_proj",
    )(x2d, g_mix, w_in, cos_t, sin_t, *later_weights)


def _attn_kernel(lam_ref, g_ref, q_ref, k_ref, v_ref, o_ref,
                 vext_ref, s0_ref, s1_ref, m0_ref, m1_ref, a0_ref, a1_ref):
    seq = q_ref.shape[0]
    tq = ATTN_TQ
    nq = seq // tq
    nt_dims = (((1,), (1,)), ((), ()))

    vext_ref[:V_DIM, :] = v_ref[...].astype(F32).T.astype(BF16)
    vext_ref[V_DIM:, :] = jnp.ones((ATTN_ONES_ROWS, seq), BF16)

    lp = lam_ref[...]
    lam = (jnp.exp(jnp.sum(lp[0:1] * lp[1:2], axis=-1, keepdims=True))
           - jnp.exp(jnp.sum(lp[2:3] * lp[3:4], axis=-1, keepdims=True)) + LAM_INIT)
    lane = lax.broadcasted_iota(jnp.int32, (tq, LANES), 1)

    bufs = ((s0_ref, m0_ref, a0_ref), (s1_ref, m1_ref, a1_ref))

    def step(qk=None, pv=None, fin=None):
        if fin is not None:
            j, (_, _, fa_ref) = fin
            acc = fa_ref[...]
            o = acc[:V_DIM] / acc[V_DIM:V_DIM + 1]
            o = (o[:, :tq] - lam * o[:, tq:]).T
            o = _rms(o) * g_ref[...] * (1.0 - LAM_INIT)
            o_ref[pl.ds(pl.multiple_of(j * tq, tq), tq), :] = o.astype(BF16)
        if qk is not None:
            i, (qs_ref, qm_ref, _) = qk
            q = q_ref[pl.ds(pl.multiple_of(i * tq, tq), tq), :]
            zero = jnp.zeros_like(q)
            q2 = jnp.concatenate([jnp.where(lane < QK_DIM, q, zero),
                                  jnp.where(lane >= QK_DIM, q, zero)], axis=0)
            m = None
        if pv is not None:
            _, (ps_ref, pm_ref, pa_ref) = pv
            m_prev = pm_ref[...]
            acc = None
        for kb in range(seq // ATTN_KB):
            rows = slice(kb * ATTN_KB, (kb + 1) * ATTN_KB)
            if qk is not None:
                s = lax.dot_general(k_ref[rows, :], q2, nt_dims, preferred_element_type=F32)
                qs_ref[rows, :] = s
                blk = jnp.max(s, axis=0, keepdims=True)
                m = blk if m is None else jnp.maximum(m, blk)
            if pv is not None:
                for sub in range(ATTN_KB // ATTN_PB):
                    r0 = kb * ATTN_KB + sub * ATTN_PB
                    rows2 = slice(r0, r0 + ATTN_PB)
                    p = jnp.exp2(ps_ref[rows2, :] - m_prev).astype(BF16)
                    part = jnp.dot(vext_ref[:, rows2], p, preferred_element_type=F32)
                    acc = part if acc is None else acc + part
        if qk is not None:
            qm_ref[...] = m
        if pv is not None:
            pa_ref[...] = acc

    def unit(t, parity):
        return t, bufs[parity]

    step(qk=unit(0, 0))
    step(qk=unit(1, 1), pv=unit(0, 0))

    def body(jj, carry):
        t = 2 * jj
        step(qk=unit(t, 0), pv=unit(t - 1, 1), fin=unit(t - 2, 0))
        step(qk=unit(t + 1, 1), pv=unit(t, 0), fin=unit(t - 1, 1))
        return carry

    lax.fori_loop(1, nq // 2, body, 0)

    step(pv=unit(nq - 1, 1), fin=unit(nq - 2, 0))
    step(fin=unit(nq - 1, 1))


def _attn(lam_params, g_sub, dq, dk, dv):
    batch, heads, seq, _ = dq.shape
    tq = ATTN_TQ
    assert (seq // tq) % 2 == 0
    seq_spec = pl.BlockSpec((None, None, seq, LANES), lambda b, h: (b, h, 0, 0))
    return pl.pallas_call(
        _attn_kernel,
        grid=(batch, heads),
        in_specs=[_const_spec((4, QK_DIM)), _const_spec((1, V_DIM)), seq_spec, seq_spec, seq_spec],
        out_specs=pl.BlockSpec((seq, V_DIM), lambda b, h: (b, h)),
        out_shape=jax.ShapeDtypeStruct((batch * seq, HEADS * V_DIM), BF16),
        scratch_shapes=[
            pltpu.VMEM((V_DIM + ATTN_ONES_ROWS, seq), BF16),
            pltpu.VMEM((seq, 2 * tq), F32), pltpu.VMEM((seq, 2 * tq), F32),
            pltpu.VMEM((1, 2 * tq), F32), pltpu.VMEM((1, 2 * tq), F32),
            pltpu.VMEM((V_DIM + ATTN_ONES_ROWS, 2 * tq), F32),
            pltpu.VMEM((V_DIM + ATTN_ONES_ROWS, 2 * tq), F32),
        ],
        compiler_params=pltpu.CompilerParams(
            dimension_semantics=("parallel", "parallel"), vmem_limit_bytes=VMEM_LIMIT),
        name="diff_attn",
    )(lam_params, g_sub, dq, dk, dv)


def _log_sigmoid(x):
    return jnp.minimum(x, 0.0) - jnp.log1p(jnp.exp(-jnp.abs(x)))


def _ret_kernel(dec_ref, q_ref, k_ref, v_ref, rg_ref, o_ref, dmat_ref, u_ref, sf_ref, sb_ref):
    seq = q_ref.shape[0]
    c_len = RET_CHUNK
    n_chunks = seq // c_len
    pair_w = 2 * V_DIM
    nt_dims = (((1,), (1,)), ((), ()))
    tn_dims = (((0,), (0,)), ((), ()))

    lg = _log_sigmoid(dec_ref[...])
    row_i = lax.broadcasted_iota(jnp.int32, (c_len, c_len), 0)
    col_i = lax.broadcasted_iota(jnp.int32, (c_len, c_len), 1)
    diff = (row_i - col_i).astype(F32)
    for hh in range(2):
        dmat_ref[hh * c_len:(hh + 1) * c_len, :] = jnp.exp(
            jnp.where(diff >= 0, diff * lg[4 + 2 * hh:5 + 2 * hh], -diff * lg[5 + 2 * hh:6 + 2 * hh]))
    rows_k = lax.broadcasted_iota(jnp.int32, (c_len, LANES), 0).astype(F32)
    rows_v = lax.broadcasted_iota(jnp.int32, (c_len, pair_w), 0).astype(F32)
    zeta_f = jnp.exp((c_len - 1 - rows_k) * lg[0:1, :LANES])
    zeta_b = jnp.exp(rows_k * lg[1:2, :LANES])
    xi_f = jnp.exp((rows_v + 1) * lg[2:3])
    xi_b = jnp.exp((c_len - rows_v) * lg[3:4])
    srow = lax.broadcasted_iota(jnp.int32, (LANES, pair_w), 0)
    scol = lax.broadcasted_iota(jnp.int32, (LANES, pair_w), 1)
    head_a_row = srow < QK_DIM
    same_head = head_a_row == (scol < V_DIM)
    g_f = jnp.exp(c_len * jnp.where(head_a_row, lg[4:5], lg[6:7]))
    g_b = jnp.exp(c_len * jnp.where(head_a_row, lg[5:6], lg[7:8]))
    lane = lax.broadcasted_iota(jnp.int32, (c_len, LANES), 1)

    def chunk(c):
        return pl.ds(pl.multiple_of(c * c_len, c_len), c_len)

    def phase_a(c, carry):
        kc = k_ref[chunk(c), :].astype(F32)
        kz = jnp.concatenate([kc * zeta_f, kc * zeta_b], axis=1).astype(BF16)
        u_ref[c] = lax.dot_general(kz, v_ref[chunk(c), :], tn_dims, preferred_element_type=F32)
        return carry

    lax.fori_loop(0, n_chunks, phase_a, 0, unroll=16)

    def phase_b(t, carry):
        s_f, s_b = carry
        sf_ref[t] = jnp.where(same_head, s_f, 0.0).astype(BF16)
        s_f = g_f * s_f + u_ref[t, :LANES, :]
        cb = n_chunks - 1 - t
        sb_ref[cb] = jnp.where(same_head, s_b, 0.0).astype(BF16)
        s_b = g_b * s_b + u_ref[cb, LANES:, :]
        return s_f, s_b

    zero_state = jnp.zeros((LANES, pair_w), F32)
    lax.fori_loop(0, n_chunks, phase_b, (zero_state, zero_state))

    def phase_c(c, carry):
        q = q_ref[chunk(c), :]
        zero = jnp.zeros_like(q)
        q2 = jnp.concatenate([jnp.where(lane < QK_DIM, q, zero),
                              jnp.where(lane >= QK_DIM, q, zero)], axis=0)
        vc = v_ref[chunk(c), :]
        sc = lax.dot_general(q2, k_ref[chunk(c), :], nt_dims, preferred_element_type=F32) * dmat_ref[...]
        inner = jnp.dot(sc.astype(BF16), vc, preferred_element_type=F32)
        tot = (jnp.concatenate([inner[:c_len, :V_DIM], inner[c_len:, V_DIM:]], axis=1)
               + jnp.dot(q, sf_ref[c], preferred_element_type=F32) * xi_f
               + jnp.dot(q, sb_ref[c], preferred_element_type=F32) * xi_b)
        r = jnp.concatenate([_rms(tot[:, :V_DIM]), _rms(tot[:, V_DIM:])], axis=1)
        gate = rg_ref[chunk(c), :].astype(F32)
        o_ref[chunk(c), :] = (r * gate * jax.nn.sigmoid(gate)).astype(BF16)
        return carry

    lax.fori_loop(0, n_chunks, phase_c, 0, unroll=16)


def _ret(dec, rq, rk, rv, rg):
    batch, pairs, seq, _ = rq.shape
    pair_w = 2 * V_DIM
    n_chunks = seq // RET_CHUNK
    assert RET_CHUNK == pair_w
    return pl.pallas_call(
        _ret_kernel,
        grid=(batch, pairs),
        in_specs=[
            pl.BlockSpec((None, 8, pair_w), lambda b, p: (p, 0, 0)),
            pl.BlockSpec((None, None, seq, LANES), lambda b, p: (b, p, 0, 0)),
            pl.BlockSpec((None, None, seq, LANES), lambda b, p: (b, p, 0, 0)),
            pl.BlockSpec((None, None, seq, pair_w), lambda b, p: (b, p, 0, 0)),
            pl.BlockSpec((None, None, seq, pair_w), lambda b, p: (b, p, 0, 0)),
        ],
        out_specs=pl.BlockSpec((seq, pair_w), lambda b, p: (b, p)),
        out_shape=jax.ShapeDtypeStruct((batch * seq, RET_V_W), BF16),
        scratch_shapes=[
            pltpu.VMEM((2 * RET_CHUNK, RET_CHUNK), F32),
            pltpu.VMEM((n_chunks, 2 * LANES, pair_w), F32),
            pltpu.VMEM((n_chunks, LANES, pair_w), BF16),
            pltpu.VMEM((n_chunks, LANES, pair_w), BF16),
        ],
        compiler_params=pltpu.CompilerParams(
            dimension_semantics=("parallel", "parallel"), vmem_limit_bytes=VMEM_LIMIT),
        name="retention",
    )(dec, rq, rk, rv, rg)


def _post_kernel(x_ref, a_ref, r_ref, ga_ref, gb_ref, wa_ref, wr_ref, wo_ref,
                 g2_ref, gf_ref, wg_ref, wu_ref, wd_ref, o_ref):
    ya = jnp.dot(a_ref[...], wa_ref[...], preferred_element_type=F32)
    yb = jnp.dot(r_ref[...], wr_ref[...], preferred_element_type=F32)
    m = (jax.nn.sigmoid(ga_ref[...].astype(F32)) * ya
         + jax.nn.sigmoid(gb_ref[...].astype(F32)) * yb)
    x = x_ref[...] + jnp.dot(m.astype(BF16), wo_ref[...], preferred_element_type=F32)
    h = (_rms(x) * g2_ref[...]).astype(BF16)
    gate = jnp.dot(h, wg_ref[...], preferred_element_type=F32)
    up = jnp.dot(h, wu_ref[...], preferred_element_type=F32)
    act = (gate * jax.nn.sigmoid(gate) * up).astype(BF16)
    x = x + jnp.dot(act, wd_ref[...], preferred_element_type=F32)
    o_ref[...] = _rms(x) * gf_ref[...]


def _post(x2d, a, r, ga, gb, wa, wr, wo, g_ffn, g_final, wg, wu, wd):
    tokens = x2d.shape[0]
    tm = POST_TM
    row = lambda w: pl.BlockSpec((tm, w), lambda i: (i, 0))
    consts = [wa, wr, wo, g_ffn, g_final, wg, wu, wd]
    return pl.pallas_call(
        _post_kernel,
        grid=(tokens // tm,),
        in_specs=[row(D_MODEL), row(DIFF_W), row(RET_V_W), row(D_MODEL), row(D_MODEL)]
        + [_const_spec(c.shape) for c in consts],
        out_specs=row(D_MODEL),
        out_shape=jax.ShapeDtypeStruct((tokens, D_MODEL), F32),
        compiler_params=pltpu.CompilerParams(
            dimension_semantics=("parallel",), vmem_limit_bytes=VMEM_LIMIT),
        name="post",
    )(x2d, a, r, ga, gb, *consts)


def _rope_tables(seq):
    half = QK_DIM // 2
    inv_freq = ROPE_THETA ** (-jnp.arange(0, QK_DIM, 2, dtype=F32) / QK_DIM)
    ang = jnp.arange(seq, dtype=F32)[:, None] * inv_freq[None, :]
    cos, sin = jnp.cos(ang), jnp.sin(ang)
    cos_t = jnp.tile(cos, (1, LANES // half))
    sin_t = jnp.tile(jnp.concatenate([-sin, sin], axis=-1), (1, LANES // QK_DIM))
    return cos_t, sin_t


def _decay_table(fwd, bwd):
    pairs = HEADS // 2
    f = fwd.astype(F32).reshape(pairs, 2)
    b = bwd.astype(F32).reshape(pairs, 2)
    k_lanes = lambda t: jnp.tile(jnp.repeat(t, QK_DIM, axis=1), (1, 2))
    v_lanes = lambda t: jnp.repeat(t, V_DIM, axis=1)
    all_lanes = lambda col: jnp.broadcast_to(col[:, None], (pairs, 2 * V_DIM))
    rows = [k_lanes(f), k_lanes(b), v_lanes(f), v_lanes(b),
            all_lanes(f[:, 0]), all_lanes(b[:, 0]), all_lanes(f[:, 1]), all_lanes(b[:, 1])]
    return jnp.stack(rows, axis=1)


def kernel(x, g_mix, w_in, diff_lq1, diff_lk1, diff_lq2, diff_lk2, diff_subln_g,
           ret_decay_fwd, ret_decay_bwd, w_up_diff, w_up_ret, w_o,
           g_ffn, w_ffn_gate, w_ffn_up, w_ffn_down, g_final):
    batch, seq, _ = x.shape
    assert w_in.shape[0] == 1, "single-layer block: LAM_INIT and the call chain are for depth 1"
    layer = 0
    x2d =x.reshape(batch * seq, D_MODEL)
    cos_t, sin_t = _rope_tables(seq)

    later = [w_up_diff[layer], w_up_ret[layer], w_o[layer], w_ffn_gate[layer], w_ffn_up[layer],
             w_ffn_down[layer]]
    (dq, dk, dv, rq, rk, rv, rg, ga, gb, wa, wr, wo, wg, wu, wd) = _proj(
        x2d, g_mix[layer][None, :], w_in[layer], cos_t, sin_t, later, batch, seq)

    lam_params = jnp.stack([diff_lq1[layer], diff_lk1[layer], diff_lq2[layer], diff_lk2[layer]])
    a = _attn(lam_params.astype(F32), diff_subln_g[layer][None, :].astype(F32), dq, dk, dv)

    r = _ret(_decay_table(ret_decay_fwd[layer], ret_decay_bwd[layer]), rq, rk, rv, rg)

    y = _post(x2d, a, r, ga, gb, wa, wr, wo, g_ffn[layer][None, :], g_final[None, :], wg, wu, wd)
    return y.reshape(batch, seq, D_MODEL)
```

```python
import math

import jax
import jax.numpy as jnp
from jax import lax
from jax.experimental import pallas as pl
from jax.experimental.pallas import tpu as pltpu

F32 = jnp.float32
BF16 = jnp.bfloat16

D_MODEL = 1024
HEADS = 4
QK_DIM = 64
V_DIM = 128
DIFF_W = HEADS * 2 * QK_DIM
RET_QK_W = HEADS * QK_DIM
RET_V_W = HEADS * V_DIM
IN_COLS = 3 * DIFF_W + 2 * RET_QK_W + 2 * RET_V_W + 2 * D_MODEL
D_FF = 2816
ROPE_THETA = 10000.0
NORM_EPS = 1e-5
LAM_INIT = 0.8 - 0.6 * math.exp(-0.3 * 0)
LOG2E = 1.4426950408889634

LANES = 128
VMEM_LIMIT = 56 * 1024 * 1024
PROJ_TM = 512
PROJ_CAST_COLS = 512
N_LATER_WEIGHTS = 6
BF16_SUBLANES = 16
ATTN_TQ = 256
ATTN_HEADS_PER_STEP = 2
ATTN_ONES_ROWS = 16
ATTN_KB = 512
ATTN_PB = 256
RET_CHUNK = 256
POST_TM = 512


def _rms(x, eps=NORM_EPS):
    return x * lax.rsqrt(jnp.mean(x * x, axis=-1, keepdims=True) + eps)


def _const_spec(shape):
    nd = len(shape)
    return pl.BlockSpec(shape, lambda *_: (0,) * nd, pipeline_mode=pl.Buffered(1))


def _proj_kernel(x_ref, g_ref, wf_ref, cos_ref, sin_ref, *refs):
    later_f32 = refs[:N_LATER_WEIGHTS]
    dq_ref, dk_ref, dv_ref, rq_ref, rk_ref, rv_ref, rg_ref, ga_ref, gb_ref = refs[N_LATER_WEIGHTS:-N_LATER_WEIGHTS - 1]
    later_bf16 = refs[-N_LATER_WEIGHTS - 1:-1]
    wb_ref = refs[-1]
    tm = x_ref.shape[0]

    @pl.when(pl.program_id(0) == 0)
    def _():
        for c in range(0, IN_COLS, PROJ_CAST_COLS):
            wb_ref[:, c:c + PROJ_CAST_COLS] = wf_ref[:, c:c + PROJ_CAST_COLS].astype(BF16)

    for src_ref, dst_ref in zip(later_f32, later_bf16):
        dst_ref[...] = src_ref[...].astype(BF16)

    hb = (_rms(x_ref[...]) * g_ref[...]).astype(BF16)
    cos = cos_ref[...]
    sin = sin_ref[...]
    lane = lax.broadcasted_iota(jnp.int32, (tm, LANES), 1)
    first_half = (lane % QK_DIM) < (QK_DIM // 2)

    def proj(c0, n):
        return jnp.dot(hb, wb_ref[:, c0:c0 + n], preferred_element_type=F32)

    def rope(y, scale):
        partner = jnp.where(first_half, pltpu.roll(y, LANES - 32, 1), pltpu.roll(y, 32, 1))
        return ((y * cos + partner * sin) * scale).astype(BF16)

    q_scale = QK_DIM ** -0.5 * LOG2E
    y = proj(0, DIFF_W)
    for h in range(HEADS):
        dq_ref[h] = rope(y[:, h * LANES:(h + 1) * LANES], q_scale)
    y = proj(DIFF_W, DIFF_W)
    for h in range(HEADS):
        dk_ref[h] = rope(y[:, h * LANES:(h + 1) * LANES], 1.0)
    y = proj(2 * DIFF_W, DIFF_W)
    for h in range(HEADS):
        dv_ref[h] = y[:, h * LANES:(h + 1) * LANES].astype(BF16)
    c0 = 3 * DIFF_W
    y = proj(c0, 2 * RET_QK_W)
    for p in range(HEADS // 2):
        rq_ref[p] = rope(y[:, p * LANES:(p + 1) * LANES], 1.0)
        rk_ref[p] = rope(y[:, RET_QK_W + p * LANES:RET_QK_W + (p + 1) * LANES], QK_DIM ** -0.5)
    c0 += 2 * RET_QK_W
    y = proj(c0, RET_V_W)
    for p in range(HEADS // 2):
        rv_ref[p] = y[:, p * 2 * V_DIM:(p + 1) * 2 * V_DIM].astype(BF16)
    c0 += RET_V_W
    y = proj(c0, RET_V_W)
    for p in range(HEADS // 2):
        rg_ref[p] = y[:, p * 2 * V_DIM:(p + 1) * 2 * V_DIM].astype(BF16)
    c0 += RET_V_W
    for half in range(2):
        ga_ref[:, half * 512:(half + 1) * 512] = proj(c0 + half * 512, 512).astype(BF16)
    c0 += D_MODEL
    for half in range(2):
        gb_ref[:, half * 512:(half + 1) * 512] = proj(c0 + half * 512, 512).astype(BF16)


def _proj(x2d, g_mix, w_in, cos_t, sin_t, later_weights, batch, seq):
    tm = PROJ_TM
    nb = seq // tm
    tokens = batch * seq
    steps = tokens // tm
    assert len(later_weights) == N_LATER_WEIGHTS
    slab_specs = []
    for w in later_weights:
        n_slabs = max(n for n in range(1, steps + 1)
                      if w.shape[0] % n == 0 and (w.shape[0] // n) % BF16_SUBLANES == 0)
        slab_specs.append(pl.BlockSpec((w.shape[0] // n_slabs, w.shape[1]),
                                       lambda i, last=n_slabs - 1: (jnp.minimum(i, last), 0)))

    def head_major(n_groups, width):
        shape = jax.ShapeDtypeStruct((batch, n_groups, seq, width), BF16)
        spec = pl.BlockSpec((None, n_groups, tm, width), lambda i: (i // nb, 0, i % nb, 0))
        return shape, spec

    outs = [head_major(HEADS, LANES)] * 3 + [head_major(HEADS // 2, LANES)] * 2 \
        + [head_major(HEADS // 2, 2 * V_DIM)] * 2
    out_shape = [o[0] for o in outs] + [jax.ShapeDtypeStruct((tokens, D_MODEL), BF16)] * 2 \
        + [jax.ShapeDtypeStruct(w.shape, BF16) for w in later_weights]
    out_specs = [o[1] for o in outs] + [pl.BlockSpec((tm, D_MODEL), lambda i: (i, 0))] * 2 + slab_specs
    return pl.pallas_call(
        _proj_kernel,
        grid=(steps,),
        in_specs=[
            pl.BlockSpec((tm, D_MODEL), lambda i: (i, 0)),
            _const_spec((1, D_MODEL)),
            _const_spec((D_MODEL, IN_COLS)),
            pl.BlockSpec((tm, LANES), lambda i: (i % nb, 0)),
            pl.BlockSpec((tm, LANES), lambda i: (i % nb, 0)),
        ] + slab_specs,
        out_specs=out_specs,
        out_shape=out_shape,
        scratch_shapes=[pltpu.VMEM((D_MODEL, IN_COLS), BF16)],
        compiler_params=pltpu.CompilerParams(
            dimension_semantics=("arbitrary",), vmem_limit_bytes=VMEM_LIMIT),
        name="in_proj",
    )(x2d, g_mix, w_in, cos_t, sin_t, *later_weights)


def _attn_kernel(lam_ref, g_ref, q_ref, k_ref, v_ref, o_ref, vext_ref, *scratch):
    n_heads, seq = q_ref.shape[0], q_ref.shape[1]
    tq = ATTN_TQ
    nq = seq // tq
    nt_dims = (((1,), (1,)), ((), ()))

    for hh in range(n_heads):
        vext_ref[hh, :V_DIM, :] = v_ref[hh].astype(F32).T.astype(BF16)
        vext_ref[hh, V_DIM:, :] = jnp.ones((ATTN_ONES_ROWS, seq), BF16)

    lp = lam_ref[...]
    lam = (jnp.exp(jnp.sum(lp[0:1] * lp[1:2], axis=-1, keepdims=True))
           - jnp.exp(jnp.sum(lp[2:3] * lp[3:4], axis=-1, keepdims=True)) + LAM_INIT)
    lane = lax.broadcasted_iota(jnp.int32, (tq, LANES), 1)

    bufs = tuple(tuple(tuple(scratch[(hh * 2 + par) * 3:(hh * 2 + par) * 3 + 3]) for par in range(2))
                 for hh in range(n_heads))
    heads = range(n_heads)

    def step(qk=None, pv=None, fin=None):
        if fin is not None:
            j, par = fin
            for hh in heads:
                acc = bufs[hh][par][2][...]
                o = acc[:V_DIM] / acc[V_DIM:V_DIM + 1]
                o = (o[:, :tq] - lam * o[:, tq:]).T
                o = _rms(o) * g_ref[...] * (1.0 - LAM_INIT)
                o_ref[pl.ds(pl.multiple_of(j * tq, tq), tq), hh * V_DIM:(hh + 1) * V_DIM] = o.astype(BF16)
        if qk is not None:
            i, qpar = qk
            q2, m = [], [None] * n_heads
            for hh in heads:
                q = q_ref[hh, pl.ds(pl.multiple_of(i * tq, tq), tq), :]
                zero = jnp.zeros_like(q)
                q2.append(jnp.concatenate([jnp.where(lane < QK_DIM, q, zero),
                                           jnp.where(lane >= QK_DIM, q, zero)], axis=0))
        if pv is not None:
            _, ppar = pv
            m_prev = [bufs[hh][ppar][1][...] for hh in heads]
            acc = [None] * n_heads
        for kb in range(seq // ATTN_KB):
            rows = slice(kb * ATTN_KB, (kb + 1) * ATTN_KB)
            for hh in heads:
                if qk is not None:
                    s = lax.dot_general(k_ref[hh, rows, :], q2[hh], nt_dims, preferred_element_type=F32)
                    bufs[hh][qpar][0][rows, :] = s
                    blk = jnp.max(s, axis=0, keepdims=True)
                    m[hh] = blk if m[hh] is None else jnp.maximum(m[hh], blk)
                if pv is not None:
                    for sub in range(ATTN_KB // ATTN_PB):
                        r0 = kb * ATTN_KB + sub * ATTN_PB
                        rows2 = slice(r0, r0 + ATTN_PB)
                        p = jnp.exp2(bufs[hh][ppar][0][rows2, :] - m_prev[hh]).astype(BF16)
                        part = jnp.dot(vext_ref[hh, :, rows2], p, preferred_element_type=F32)
                        acc[hh] = part if acc[hh] is None else acc[hh] + part
        for hh in heads:
            if qk is not None:
                bufs[hh][qpar][1][...] = m[hh]
            if pv is not None:
                bufs[hh][ppar][2][...] = acc[hh]

    def unit(t, parity):
        return t, parity

    step(qk=unit(0, 0))
    step(qk=unit(1, 1), pv=unit(0, 0))

    def body(jj, carry):
        t = 2 * jj
        step(qk=unit(t, 0), pv=unit(t - 1, 1), fin=unit(t - 2, 0))
        step(qk=unit(t + 1, 1), pv=unit(t, 0), fin=unit(t - 1, 1))
        return carry

    lax.fori_loop(1, nq // 2, body, 0)

    step(pv=unit(nq - 1, 1), fin=unit(nq - 2, 0))
    step(fin=unit(nq - 1, 1))


def _attn(lam_params, g_sub, dq, dk, dv):
    batch, heads, seq, _ = dq.shape
    tq = ATTN_TQ
    hp = ATTN_HEADS_PER_STEP
    assert (seq // tq) % 2 == 0 and heads % hp == 0
    seq_spec = pl.BlockSpec((None, hp, seq, LANES), lambda b, p: (b, p, 0, 0))
    unit_bufs = [pltpu.VMEM((seq, 2 * tq), F32), pltpu.VMEM((1, 2 * tq), F32),
                 pltpu.VMEM((V_DIM + ATTN_ONES_ROWS, 2 * tq), F32)]
    return pl.pallas_call(
        _attn_kernel,
        grid=(batch, heads // hp),
        in_specs=[_const_spec((4, QK_DIM)), _const_spec((1, V_DIM)), seq_spec, seq_spec, seq_spec],
        out_specs=pl.BlockSpec((seq, hp * V_DIM), lambda b, p: (b, p)),
        out_shape=jax.ShapeDtypeStruct((batch * seq, HEADS * V_DIM), BF16),
        scratch_shapes=[pltpu.VMEM((hp, V_DIM + ATTN_ONES_ROWS, seq), BF16)] + unit_bufs * (2 * hp),
        compiler_params=pltpu.CompilerParams(
            dimension_semantics=("parallel", "parallel"), vmem_limit_bytes=VMEM_LIMIT),
        name="diff_attn",
    )(lam_params, g_sub, dq, dk, dv)


def _log_sigmoid(x):
    return jnp.minimum(x, 0.0) - jnp.log1p(jnp.exp(-jnp.abs(x)))


def _ret_kernel(dec_ref, q_ref, k_ref, v_ref, rg_ref, o_ref, dmat_ref, u_ref, sf_ref, sb_ref):
    seq = q_ref.shape[0]
    c_len = RET_CHUNK
    n_chunks = seq // c_len
    pair_w = 2 * V_DIM
    nt_dims = (((1,), (1,)), ((), ()))
    tn_dims = (((0,), (0,)), ((), ()))

    lg = _log_sigmoid(dec_ref[...])
    row_i = lax.broadcasted_iota(jnp.int32, (c_len, c_len), 0)
    col_i = lax.broadcasted_iota(jnp.int32, (c_len, c_len), 1)
    diff = (row_i - col_i).astype(F32)
    for hh in range(2):
        dmat_ref[hh * c_len:(hh + 1) * c_len, :] = jnp.exp(
            jnp.where(diff >= 0, diff * lg[4 + 2 * hh:5 + 2 * hh], -diff * lg[5 + 2 * hh:6 + 2 * hh]))
    rows_k = lax.broadcasted_iota(jnp.int32, (c_len, LANES), 0).astype(F32)
    rows_v = lax.broadcasted_iota(jnp.int32, (c_len, pair_w), 0).astype(F32)
    zeta_f = jnp.exp((c_len - 1 - rows_k) * lg[0:1, :LANES])
    zeta_b = jnp.exp(rows_k * lg[1:2, :LANES])
    xi_f = jnp.exp((rows_v + 1) * lg[2:3])
    xi_b = jnp.exp((c_len - rows_v) * lg[3:4])
    srow = lax.broadcasted_iota(jnp.int32, (LANES, pair_w), 0)
    scol = lax.broadcasted_iota(jnp.int32, (LANES, pair_w), 1)
    head_a_row = srow < QK_DIM
    same_head = head_a_row == (scol < V_DIM)
    g_f = jnp.exp(c_len * jnp.where(head_a_row, lg[4:5], lg[6:7]))
    g_b = jnp.exp(c_len * jnp.where(head_a_row, lg[5:6], lg[7:8]))
    lane = lax.broadcasted_iota(jnp.int32, (c_len, LANES), 1)

    def chunk(c):
        return pl.ds(pl.multiple_of(c * c_len, c_len), c_len)

    def phase_a(c, carry):
        kc = k_ref[chunk(c), :].astype(F32)
        kz = jnp.concatenate([kc * zeta_f, kc * zeta_b], axis=1).astype(BF16)
        u_ref[c] = lax.dot_general(kz, v_ref[chunk(c), :], tn_dims, preferred_element_type=F32)
        return carry

    lax.fori_loop(0, n_chunks, phase_a, 0, unroll=16)

    def phase_b(t, carry):
        s_f, s_b = carry
        sf_ref[t] = jnp.where(same_head, s_f, 0.0).astype(BF16)
        s_f = g_f * s_f + u_ref[t, :LANES, :]
        cb = n_chunks - 1 - t
        sb_ref[cb] = jnp.where(same_head, s_b, 0.0).astype(BF16)
        s_b = g_b * s_b + u_ref[cb, LANES:, :]
        return s_f, s_b

    zero_state = jnp.zeros((LANES, pair_w), F32)
    lax.fori_loop(0, n_chunks, phase_b, (zero_state, zero_state))

    def phase_c(c, carry):
        q = q_ref[chunk(c), :]
        zero = jnp.zeros_like(q)
        q2 = jnp.concatenate([jnp.where(lane < QK_DIM, q, zero),
                              jnp.where(lane >= QK_DIM, q, zero)], axis=0)
        vc = v_ref[chunk(c), :]
        sc = lax.dot_general(q2, k_ref[chunk(c), :], nt_dims, preferred_element_type=F32) * dmat_ref[...]
        inner = jnp.dot(sc.astype(BF16), vc, preferred_element_type=F32)
        tot = (jnp.concatenate([inner[:c_len, :V_DIM], inner[c_len:, V_DIM:]], axis=1)
               + jnp.dot(q, sf_ref[c], preferred_element_type=F32) * xi_f
               + jnp.dot(q, sb_ref[c], preferred_element_type=F32) * xi_b)
        r = jnp.concatenate([_rms(tot[:, :V_DIM]), _rms(tot[:, V_DIM:])], axis=1)
        gate = rg_ref[chunk(c), :].astype(F32)
        o_ref[chunk(c), :] = (r * gate * jax.nn.sigmoid(gate)).astype(BF16)
        return carry

    lax.fori_loop(0, n_chunks, phase_c, 0, unroll=16)


def _ret(dec, rq, rk, rv, rg):
    batch, pairs, seq, _ = rq.shape
    pair_w = 2 * V_DIM
    n_chunks = seq // RET_CHUNK
    assert RET_CHUNK == pair_w
    return pl.pallas_call(
        _ret_kernel,
        grid=(batch, pairs),
        in_specs=[
            pl.BlockSpec((None, 8, pair_w), lambda b, p: (p, 0, 0)),
            pl.BlockSpec((None, None, seq, LANES), lambda b, p: (b, p, 0, 0)),
            pl.BlockSpec((None, None, seq, LANES), lambda b, p: (b, p, 0, 0)),
            pl.BlockSpec((None, None, seq, pair_w), lambda b, p: (b, p, 0, 0)),
            pl.BlockSpec((None, None, seq, pair_w), lambda b, p: (b, p, 0, 0)),
        ],
        out_specs=pl.BlockSpec((seq, pair_w), lambda b, p: (b, p)),
        out_shape=jax.ShapeDtypeStruct((batch * seq, RET_V_W), BF16),
        scratch_shapes=[
            pltpu.VMEM((2 * RET_CHUNK, RET_CHUNK), F32),
            pltpu.VMEM((n_chunks, 2 * LANES, pair_w), F32),
            pltpu.VMEM((n_chunks, LANES, pair_w), BF16),
            pltpu.VMEM((n_chunks, LANES, pair_w), BF16),
        ],
        compiler_params=pltpu.CompilerParams(
            dimension_semantics=("parallel", "parallel"), vmem_limit_bytes=VMEM_LIMIT),
        name="retention",
    )(dec, rq, rk, rv, rg)


def _post_kernel(x_ref, a_ref, r_ref, ga_ref, gb_ref, wa_ref, wr_ref, wo_ref,
                 g2_ref, gf_ref, wg_ref, wu_ref, wd_ref, o_ref):
    ya = jnp.dot(a_ref[...], wa_ref[...], preferred_element_type=F32)
    yb = jnp.dot(r_ref[...], wr_ref[...], preferred_element_type=F32)
    m = (jax.nn.sigmoid(ga_ref[...].astype(F32)) * ya
         + jax.nn.sigmoid(gb_ref[...].astype(F32)) * yb)
    x = x_ref[...] + jnp.dot(m.astype(BF16), wo_ref[...], preferred_element_type=F32)
    h = (_rms(x) * g2_ref[...]).astype(BF16)
    gate = jnp.dot(h, wg_ref[...], preferred_element_type=F32)
    up = jnp.dot(h, wu_ref[...], preferred_element_type=F32)
    act = (gate * jax.nn.sigmoid(gate) * up).astype(BF16)
    x = x + jnp.dot(act, wd_ref[...], preferred_element_type=F32)
    o_ref[...] = _rms(x) * gf_ref[...]


def _post(x2d, a, r, ga, gb, wa, wr, wo, g_ffn, g_final, wg, wu, wd):
    tokens = x2d.shape[0]
    tm = POST_TM
    row = lambda w: pl.BlockSpec((tm, w), lambda i: (i, 0))
    consts = [wa, wr, wo, g_ffn, g_final, wg, wu, wd]
    return pl.pallas_call(
        _post_kernel,
        grid=(tokens // tm,),
        in_specs=[row(D_MODEL), row(DIFF_W), row(RET_V_W), row(D_MODEL), row(D_MODEL)]
        + [_const_spec(c.shape) for c in consts],
        out_specs=row(D_MODEL),
        out_shape=jax.ShapeDtypeStruct((tokens, D_MODEL), F32),
        compiler_params=pltpu.CompilerParams(
            dimension_semantics=("parallel",), vmem_limit_bytes=VMEM_LIMIT),
        name="post",
    )(x2d, a, r, ga, gb, *consts)


def _rope_tables(seq):
    half = QK_DIM // 2
    inv_freq = ROPE_THETA ** (-jnp.arange(0, QK_DIM, 2, dtype=F32) / QK_DIM)
    ang = jnp.arange(seq, dtype=F32)[:, None] * inv_freq[None, :]
    cos, sin = jnp.cos(ang), jnp.sin(ang)
    cos_t = jnp.tile(cos, (1, LANES // half))
    sin_t = jnp.tile(jnp.concatenate([-sin, sin], axis=-1), (1, LANES // QK_DIM))
    return cos_t, sin_t


def _decay_table(fwd, bwd):
    pairs = HEADS // 2
    f = fwd.astype(F32).reshape(pairs, 2)
    b = bwd.astype(F32).reshape(pairs, 2)
    k_lanes = lambda t: jnp.tile(jnp.repeat(t, QK_DIM, axis=1), (1, 2))
    v_lanes = lambda t: jnp.repeat(t, V_DIM, axis=1)
    all_lanes = lambda col: jnp.broadcast_to(col[:, None], (pairs, 2 * V_DIM))
    rows = [k_lanes(f), k_lanes(b), v_lanes(f), v_lanes(b),
            all_lanes(f[:, 0]), all_lanes(b[:, 0]), all_lanes(f[:, 1]), all_lanes(b[:, 1])]
    return jnp.stack(rows, axis=1)


def kernel(x, g_mix, w_in, diff_lq1, diff_lk1, diff_lq2, diff_lk2, diff_subln_g,
           ret_decay_fwd, ret_decay_bwd, w_up_diff, w_up_ret, w_o,
           g_ffn, w_ffn_gate, w_ffn_up, w_ffn_down, g_final):
    batch, seq, _ = x.shape
    assert w_in.shape[0] == 1, "single-layer block: LAM_INIT and the call chain are for depth 1"
    layer = 0
    x2d =x.reshape(batch * seq, D_MODEL)
    cos_t, sin_t = _rope_tables(seq)

    later = [w_up_diff[layer], w_up_ret[layer], w_o[layer], w_ffn_gate[layer], w_ffn_up[layer],
             w_ffn_down[layer]]
    (dq, dk, dv, rq, rk, rv, rg, ga, gb, wa, wr, wo, wg, wu, wd) = _proj(
        x2d, g_mix[layer][None, :], w_in[layer], cos_t, sin_t, later, batch, seq)

    lam_params = jnp.stack([diff_lq1[layer], diff_lk1[layer], diff_lq2[layer], diff_lk2[layer]])
    a = _attn(lam_params.astype(F32), diff_subln_g[layer][None, :].astype(F32), dq, dk, dv)

    r = _ret(_decay_table(ret_decay_fwd[layer], ret_decay_bwd[layer]), rq, rk, rv, rg)

    y = _post(x2d, a, r, ga, gb, wa, wr, wo, g_ffn[layer][None, :], g_final[None, :], wg, wu, wd)
    return y.reshape(batch, seq, D_MODEL)
```

```python
import math

import jax
import jax.numpy as jnp
from jax import lax
from jax.experimental import pallas as pl
from jax.experimental.pallas import tpu as pltpu

F32 = jnp.float32
BF16 = jnp.bfloat16

D_MODEL = 1024
HEADS = 4
QK_DIM = 64
V_DIM = 128
DIFF_W = HEADS * 2 * QK_DIM
RET_QK_W = HEADS * QK_DIM
RET_V_W = HEADS * V_DIM
IN_COLS = 3 * DIFF_W + 2 * RET_QK_W + 2 * RET_V_W + 2 * D_MODEL
D_FF = 2816
ROPE_THETA = 10000.0
NORM_EPS = 1e-5
LAM_INIT = 0.8 - 0.6 * math.exp(-0.3 * 0)
LOG2E = 1.4426950408889634

LANES = 128
VMEM_LIMIT = 56 * 1024 * 1024
PROJ_TM = 512
PROJ_CAST_COLS = 512
N_LATER_WEIGHTS = 6
BF16_SUBLANES = 16
ATTN_TQ = 256
ATTN_HEADS_PER_STEP = 2
ATTN_ONES_ROWS = 16
ATTN_KB = 512
ATTN_PB = 256
RET_CHUNK = 256
POST_TM = 512


def _rms(x, eps=NORM_EPS):
    return x * lax.rsqrt(jnp.mean(x * x, axis=-1, keepdims=True) + eps)


def _const_spec(shape):
    nd = len(shape)
    return pl.BlockSpec(shape, lambda *_: (0,) * nd, pipeline_mode=pl.Buffered(1))


def _proj_kernel(x_ref, g_ref, wf_ref, cos_ref, sin_ref, *refs):
    later_f32 = refs[:N_LATER_WEIGHTS]
    dq_ref, dk_ref, dv_ref, rq_ref, rk_ref, rv_ref, rg_ref, ga_ref, gb_ref = refs[N_LATER_WEIGHTS:-N_LATER_WEIGHTS - 1]
    later_bf16 = refs[-N_LATER_WEIGHTS - 1:-1]
    wb_ref = refs[-1]
    tm = x_ref.shape[0]

    @pl.when(pl.program_id(0) == 0)
    def _():
        for c in range(0, IN_COLS, PROJ_CAST_COLS):
            wb_ref[:, c:c + PROJ_CAST_COLS] = wf_ref[:, c:c + PROJ_CAST_COLS].astype(BF16)

    for src_ref, dst_ref in zip(later_f32, later_bf16):
        dst_ref[...] = src_ref[...].astype(BF16)

    x = x_ref[...]
    hb = (x * g_ref[...]).astype(BF16)
    inv = lax.rsqrt(jnp.mean(x * x, axis=-1, keepdims=True) + NORM_EPS)
    cos = cos_ref[...]
    sin = sin_ref[...]
    lane = lax.broadcasted_iota(jnp.int32, (tm, LANES), 1)
    first_half = (lane % QK_DIM) < (QK_DIM // 2)

    def proj(c0, n):
        return jnp.dot(hb, wb_ref[:, c0:c0 + n], preferred_element_type=F32) * inv

    def rope(y, scale):
        partner = jnp.where(first_half, pltpu.roll(y, LANES - 32, 1), pltpu.roll(y, 32, 1))
        return ((y * cos + partner * sin) * scale).astype(BF16)

    q_scale = QK_DIM ** -0.5 * LOG2E
    y = proj(0, DIFF_W)
    for h in range(HEADS):
        dq_ref[h] = rope(y[:, h * LANES:(h + 1) * LANES], q_scale)
    y = proj(DIFF_W, DIFF_W)
    for h in range(HEADS):
        dk_ref[h] = rope(y[:, h * LANES:(h + 1) * LANES], 1.0)
    y = proj(2 * DIFF_W, DIFF_W)
    for h in range(HEADS):
        dv_ref[h] = y[:, h * LANES:(h + 1) * LANES].astype(BF16)
    c0 = 3 * DIFF_W
    y = proj(c0, 2 * RET_QK_W)
    for p in range(HEADS // 2):
        rq_ref[p] = rope(y[:, p * LANES:(p + 1) * LANES], 1.0)
        rk_ref[p] = rope(y[:, RET_QK_W + p * LANES:RET_QK_W + (p + 1) * LANES], QK_DIM ** -0.5)
    c0 += 2 * RET_QK_W
    y = proj(c0, RET_V_W)
    for p in range(HEADS // 2):
        rv_ref[p] = y[:, p * 2 * V_DIM:(p + 1) * 2 * V_DIM].astype(BF16)
    c0 += RET_V_W
    y = proj(c0, RET_V_W)
    for p in range(HEADS // 2):
        rg_ref[p] = y[:, p * 2 * V_DIM:(p + 1) * 2 * V_DIM].astype(BF16)
    c0 += RET_V_W
    for half in range(2):
        ga_ref[:, half * 512:(half + 1) * 512] = proj(c0 + half * 512, 512).astype(BF16)
    c0 += D_MODEL
    for half in range(2):
        gb_ref[:, half * 512:(half + 1) * 512] = proj(c0 + half * 512, 512).astype(BF16)


def _proj(x2d, g_mix, w_in, cos_t, sin_t, later_weights, batch, seq):
    tm = PROJ_TM
    nb = seq // tm
    tokens = batch * seq
    steps = tokens // tm
    assert len(later_weights) == N_LATER_WEIGHTS
    slab_specs = []
    for w in later_weights:
        n_slabs = max(n for n in range(1, steps + 1)
                      if w.shape[0] % n == 0 and (w.shape[0] // n) % BF16_SUBLANES == 0)
        slab_specs.append(pl.BlockSpec((w.shape[0] // n_slabs, w.shape[1]),
                                       lambda i, last=n_slabs - 1: (jnp.minimum(i, last), 0)))

    def head_major(n_groups, width):
        shape = jax.ShapeDtypeStruct((batch, n_groups, seq, width), BF16)
        spec = pl.BlockSpec((None, n_groups, tm, width), lambda i: (i // nb, 0, i % nb, 0))
        return shape, spec

    outs = [head_major(HEADS, LANES)] * 3 + [head_major(HEADS // 2, LANES)] * 2 \
        + [head_major(HEADS // 2, 2 * V_DIM)] * 2
    out_shape = [o[0] for o in outs] + [jax.ShapeDtypeStruct((tokens, D_MODEL), BF16)] * 2 \
        + [jax.ShapeDtypeStruct(w.shape, BF16) for w in later_weights]
    out_specs = [o[1] for o in outs] + [pl.BlockSpec((tm, D_MODEL), lambda i: (i, 0))] * 2 + slab_specs
    return pl.pallas_call(
        _proj_kernel,
        grid=(steps,),
        in_specs=[
            pl.BlockSpec((tm, D_MODEL), lambda i: (i, 0)),
            _const_spec((1, D_MODEL)),
            _const_spec((D_MODEL, IN_COLS)),
            pl.BlockSpec((tm, LANES), lambda i: (i % nb, 0)),
            pl.BlockSpec((tm, LANES), lambda i: (i % nb, 0)),
        ] + slab_specs,
        out_specs=out_specs,
        out_shape=out_shape,
        scratch_shapes=[pltpu.VMEM((D_MODEL, IN_COLS), BF16)],
        compiler_params=pltpu.CompilerParams(
            dimension_semantics=("arbitrary",), vmem_limit_bytes=VMEM_LIMIT),
        name="in_proj",
    )(x2d, g_mix, w_in, cos_t, sin_t, *later_weights)


def _attn_kernel(lam_ref, g_ref, q_ref, k_ref, v_ref, o_ref, vext_ref, *scratch):
    n_heads, seq = q_ref.shape[0], q_ref.shape[1]
    tq = ATTN_TQ
    nq = seq // tq
    nt_dims = (((1,), (1,)), ((), ()))

    for hh in range(n_heads):
        vext_ref[hh, :V_DIM, :] = v_ref[hh].astype(F32).T.astype(BF16)
        vext_ref[hh, V_DIM:, :] = jnp.ones((ATTN_ONES_ROWS, seq), BF16)

    lp = lam_ref[...]
    lam = (jnp.exp(jnp.sum(lp[0:1] * lp[1:2], axis=-1, keepdims=True))
           - jnp.exp(jnp.sum(lp[2:3] * lp[3:4], axis=-1, keepdims=True)) + LAM_INIT)
    lane = lax.broadcasted_iota(jnp.int32, (tq, LANES), 1)

    bufs = tuple(tuple(tuple(scratch[(hh * 2 + par) * 3:(hh * 2 + par) * 3 + 3]) for par in range(2))
                 for hh in range(n_heads))
    heads = range(n_heads)

    def step(qk=None, pv=None, fin=None):
        if fin is not None:
            j, par = fin
            for hh in heads:
                acc = bufs[hh][par][2][...]
                o = acc[:V_DIM] / acc[V_DIM:V_DIM + 1]
                o = (o[:, :tq] - lam * o[:, tq:]).T
                o = _rms(o) * g_ref[...] * (1.0 - LAM_INIT)
                o_ref[pl.ds(pl.multiple_of(j * tq, tq), tq), hh * V_DIM:(hh + 1) * V_DIM] = o.astype(BF16)
        if qk is not None:
            i, qpar = qk
            q2, m = [], [None] * n_heads
            for hh in heads:
                q = q_ref[hh, pl.ds(pl.multiple_of(i * tq, tq), tq), :]
                zero = jnp.zeros_like(q)
                q2.append(jnp.concatenate([jnp.where(lane < QK_DIM, q, zero),
                                           jnp.where(lane >= QK_DIM, q, zero)], axis=0))
        if pv is not None:
            _, ppar = pv
            m_prev = [bufs[hh][ppar][1][...] for hh in heads]
            acc = [None] * n_heads
        for kb in range(seq // ATTN_KB):
            rows = slice(kb * ATTN_KB, (kb + 1) * ATTN_KB)
            for hh in heads:
                if qk is not None:
                    s = lax.dot_general(k_ref[hh, rows, :], q2[hh], nt_dims, preferred_element_type=F32)
                    bufs[hh][qpar][0][rows, :] = s
                    blk = jnp.max(s, axis=0, keepdims=True)
                    m[hh] = blk if m[hh] is None else jnp.maximum(m[hh], blk)
                if pv is not None:
                    for sub in range(ATTN_KB // ATTN_PB):
                        r0 = kb * ATTN_KB + sub * ATTN_PB
                        rows2 = slice(r0, r0 + ATTN_PB)
                        p = jnp.exp2(bufs[hh][ppar][0][rows2, :] - m_prev[hh]).astype(BF16)
                        part = jnp.dot(vext_ref[hh, :, rows2], p, preferred_element_type=F32)
                        acc[hh] = part if acc[hh] is None else acc[hh] + part
        for hh in heads:
            if qk is not None:
                bufs[hh][qpar][1][...] = m[hh]
            if pv is not None:
                bufs[hh][ppar][2][...] = acc[hh]

    def unit(t, parity):
        return t, parity

    step(qk=unit(0, 0))
    step(qk=unit(1, 1), pv=unit(0, 0))

    def body(jj, carry):
        t = 2 * jj
        step(qk=unit(t, 0), pv=unit(t - 1, 1), fin=unit(t - 2, 0))
        step(qk=unit(t + 1, 1), pv=unit(t, 0), fin=unit(t - 1, 1))
        return carry

    lax.fori_loop(1, nq // 2, body, 0)

    step(pv=unit(nq - 1, 1), fin=unit(nq - 2, 0))
    step(fin=unit(nq - 1, 1))


def _attn(lam_params, g_sub, dq, dk, dv):
    batch, heads, seq, _ = dq.shape
    tq = ATTN_TQ
    hp = ATTN_HEADS_PER_STEP
    assert (seq // tq) % 2 == 0 and heads % hp == 0
    seq_spec = pl.BlockSpec((None, hp, seq, LANES), lambda b, p: (b, p, 0, 0))
    unit_bufs = [pltpu.VMEM((seq, 2 * tq), F32), pltpu.VMEM((1, 2 * tq), F32),
                 pltpu.VMEM((V_DIM + ATTN_ONES_ROWS, 2 * tq), F32)]
    return pl.pallas_call(
        _attn_kernel,
        grid=(batch, heads // hp),
        in_specs=[_const_spec((4, QK_DIM)), _const_spec((1, V_DIM)), seq_spec, seq_spec, seq_spec],
        out_specs=pl.BlockSpec((seq, hp * V_DIM), lambda b, p: (b, p)),
        out_shape=jax.ShapeDtypeStruct((batch * seq, HEADS * V_DIM), BF16),
        scratch_shapes=[pltpu.VMEM((hp, V_DIM + ATTN_ONES_ROWS, seq), BF16)] + unit_bufs * (2 * hp),
        compiler_params=pltpu.CompilerParams(
            dimension_semantics=("parallel", "parallel"), vmem_limit_bytes=VMEM_LIMIT),
        name="diff_attn",
    )(lam_params, g_sub, dq, dk, dv)


def _log_sigmoid(x):
    return jnp.minimum(x, 0.0) - jnp.log1p(jnp.exp(-jnp.abs(x)))


def _ret_kernel(dec_ref, q_ref, k_ref, v_ref, rg_ref, o_ref, dmat_ref, u_ref, sf_ref, sb_ref):
    seq = q_ref.shape[0]
    c_len = RET_CHUNK
    n_chunks = seq // c_len
    pair_w = 2 * V_DIM
    nt_dims = (((1,), (1,)), ((), ()))
    tn_dims = (((0,), (0,)), ((), ()))

    lg = _log_sigmoid(dec_ref[...])
    row_i = lax.broadcasted_iota(jnp.int32, (c_len, c_len), 0)
    col_i = lax.broadcasted_iota(jnp.int32, (c_len, c_len), 1)
    diff = (row_i - col_i).astype(F32)
    for hh in range(2):
        dmat_ref[hh * c_len:(hh + 1) * c_len, :] = jnp.exp(
            jnp.where(diff >= 0, diff * lg[4 + 2 * hh:5 + 2 * hh], -diff * lg[5 + 2 * hh:6 + 2 * hh]))
    rows_k = lax.broadcasted_iota(jnp.int32, (c_len, LANES), 0).astype(F32)
    rows_v = lax.broadcasted_iota(jnp.int32, (c_len, pair_w), 0).astype(F32)
    zeta_f = jnp.exp((c_len - 1 - rows_k) * lg[0:1, :LANES])
    zeta_b = jnp.exp(rows_k * lg[1:2, :LANES])
    xi_f = jnp.exp((rows_v + 1) * lg[2:3])
    xi_b = jnp.exp((c_len - rows_v) * lg[3:4])
    srow = lax.broadcasted_iota(jnp.int32, (LANES, pair_w), 0)
    scol = lax.broadcasted_iota(jnp.int32, (LANES, pair_w), 1)
    head_a_row = srow < QK_DIM
    same_head = head_a_row == (scol < V_DIM)
    g_f = jnp.exp(c_len * jnp.where(head_a_row, lg[4:5], lg[6:7]))
    g_b = jnp.exp(c_len * jnp.where(head_a_row, lg[5:6], lg[7:8]))
    lane = lax.broadcasted_iota(jnp.int32, (c_len, LANES), 1)

    def chunk(c):
        return pl.ds(pl.multiple_of(c * c_len, c_len), c_len)

    def phase_a(c, carry):
        kc = k_ref[chunk(c), :].astype(F32)
        kz = jnp.concatenate([kc * zeta_f, kc * zeta_b], axis=1).astype(BF16)
        u_ref[c] = lax.dot_general(kz, v_ref[chunk(c), :], tn_dims, preferred_element_type=F32)
        return carry

    lax.fori_loop(0, n_chunks, phase_a, 0, unroll=16)

    def phase_b(t, carry):
        s_f, s_b = carry
        sf_ref[t] = jnp.where(same_head, s_f, 0.0).astype(BF16)
        s_f = g_f * s_f + u_ref[t, :LANES, :]
        cb = n_chunks - 1 - t
        sb_ref[cb] = jnp.where(same_head, s_b, 0.0).astype(BF16)
        s_b = g_b * s_b + u_ref[cb, LANES:, :]
        return s_f, s_b

    zero_state = jnp.zeros((LANES, pair_w), F32)
    lax.fori_loop(0, n_chunks, phase_b, (zero_state, zero_state))

    def phase_c(c, carry):
        q = q_ref[chunk(c), :]
        zero = jnp.zeros_like(q)
        q2 = jnp.concatenate([jnp.where(lane < QK_DIM, q, zero),
                              jnp.where(lane >= QK_DIM, q, zero)], axis=0)
        vc = v_ref[chunk(c), :]
        sc = lax.dot_general(q2, k_ref[chunk(c), :], nt_dims, preferred_element_type=F32) * dmat_ref[...]
        inner = jnp.dot(sc.astype(BF16), vc, preferred_element_type=F32)
        tot = (jnp.concatenate([inner[:c_len, :V_DIM], inner[c_len:, V_DIM:]], axis=1)
               + jnp.dot(q, sf_ref[c], preferred_element_type=F32) * xi_f
               + jnp.dot(q, sb_ref[c], preferred_element_type=F32) * xi_b)
        r = jnp.concatenate([_rms(tot[:, :V_DIM]), _rms(tot[:, V_DIM:])], axis=1)
        gate = rg_ref[chunk(c), :].astype(F32)
        o_ref[chunk(c), :] = (r * gate * jax.nn.sigmoid(gate)).astype(BF16)
        return carry

    lax.fori_loop(0, n_chunks, phase_c, 0, unroll=16)


def _ret(dec, rq, rk, rv, rg):
    batch, pairs, seq, _ = rq.shape
    pair_w = 2 * V_DIM
    n_chunks = seq // RET_CHUNK
    assert RET_CHUNK == pair_w
    return pl.pallas_call(
        _ret_kernel,
        grid=(batch, pairs),
        in_specs=[
            pl.BlockSpec((None, 8, pair_w), lambda b, p: (p, 0, 0)),
            pl.BlockSpec((None, None, seq, LANES), lambda b, p: (b, p, 0, 0)),
            pl.BlockSpec((None, None, seq, LANES), lambda b, p: (b, p, 0, 0)),
            pl.BlockSpec((None, None, seq, pair_w), lambda b, p: (b, p, 0, 0)),
            pl.BlockSpec((None, None, seq, pair_w), lambda b, p: (b, p, 0, 0)),
        ],
        out_specs=pl.BlockSpec((seq, pair_w), lambda b, p: (b, p)),
        out_shape=jax.ShapeDtypeStruct((batch * seq, RET_V_W), BF16),
        scratch_shapes=[
            pltpu.VMEM((2 * RET_CHUNK, RET_CHUNK), F32),
            pltpu.VMEM((n_chunks, 2 * LANES, pair_w), F32),
            pltpu.VMEM((n_chunks, LANES, pair_w), BF16),
            pltpu.VMEM((n_chunks, LANES, pair_w), BF16),
        ],
        compiler_params=pltpu.CompilerParams(
            dimension_semantics=("parallel", "parallel"), vmem_limit_bytes=VMEM_LIMIT),
        name="retention",
    )(dec, rq, rk, rv, rg)


def _post_kernel(x_ref, a_ref, r_ref, ga_ref, gb_ref, wa_ref, wr_ref, wo_ref,
                 g2_ref, gf_ref, wg_ref, wu_ref, wd_ref, o_ref):
    ya = jnp.dot(a_ref[...], wa_ref[...], preferred_element_type=F32)
    yb = jnp.dot(r_ref[...], wr_ref[...], preferred_element_type=F32)
    m = (jax.nn.sigmoid(ga_ref[...].astype(F32)) * ya
         + jax.nn.sigmoid(gb_ref[...].astype(F32)) * yb)
    x = x_ref[...] + jnp.dot(m.astype(BF16), wo_ref[...], preferred_element_type=F32)
    inv = lax.rsqrt(jnp.mean(x * x, axis=-1, keepdims=True) + NORM_EPS)
    h = (x * g2_ref[...]).astype(BF16)
    gate = jnp.dot(h, wg_ref[...], preferred_element_type=F32) * inv
    up = jnp.dot(h, wu_ref[...], preferred_element_type=F32) * inv
    act = (gate * jax.nn.sigmoid(gate) * up).astype(BF16)
    x = x + jnp.dot(act, wd_ref[...], preferred_element_type=F32)
    o_ref[...] = _rms(x) * gf_ref[...]


def _post(x2d, a, r, ga, gb, wa, wr, wo, g_ffn, g_final, wg, wu, wd):
    tokens = x2d.shape[0]
    tm = POST_TM
    row = lambda w: pl.BlockSpec((tm, w), lambda i: (i, 0))
    consts = [wa, wr, wo, g_ffn, g_final, wg, wu, wd]
    return pl.pallas_call(
        _post_kernel,
        grid=(tokens // tm,),
        in_specs=[row(D_MODEL), row(DIFF_W), row(RET_V_W), row(D_MODEL), row(D_MODEL)]
        + [_const_spec(c.shape) for c in consts],
        out_specs=row(D_MODEL),
        out_shape=jax.ShapeDtypeStruct((tokens, D_MODEL), F32),
        compiler_params=pltpu.CompilerParams(
            dimension_semantics=("parallel",), vmem_limit_bytes=VMEM_LIMIT),
        name="post",
    )(x2d, a, r, ga, gb, *consts)


def _rope_tables(seq):
    half = QK_DIM // 2
    inv_freq = ROPE_THETA ** (-jnp.arange(0, QK_DIM, 2, dtype=F32) / QK_DIM)
    ang = jnp.arange(seq, dtype=F32)[:, None] * inv_freq[None, :]
    cos, sin = jnp.cos(ang), jnp.sin(ang)
    cos_t = jnp.tile(cos, (1, LANES // half))
    sin_t = jnp.tile(jnp.concatenate([-sin, sin], axis=-1), (1, LANES // QK_DIM))
    return cos_t, sin_t


def _decay_table(fwd, bwd):
    pairs = HEADS // 2
    f = fwd.astype(F32).reshape(pairs, 2)
    b = bwd.astype(F32).reshape(pairs, 2)
    k_lanes = lambda t: jnp.tile(jnp.repeat(t, QK_DIM, axis=1), (1, 2))
    v_lanes = lambda t: jnp.repeat(t, V_DIM, axis=1)
    all_lanes = lambda col: jnp.broadcast_to(col[:, None], (pairs, 2 * V_DIM))
    rows = [k_lanes(f), k_lanes(b), v_lanes(f), v_lanes(b),
            all_lanes(f[:, 0]), all_lanes(b[:, 0]), all_lanes(f[:, 1]), all_lanes(b[:, 1])]
    return jnp.stack(rows, axis=1)


def kernel(x, g_mix, w_in, diff_lq1, diff_lk1, diff_lq2, diff_lk2, diff_subln_g,
           ret_decay_fwd, ret_decay_bwd, w_up_diff, w_up_ret, w_o,
           g_ffn, w_ffn_gate, w_ffn_up, w_ffn_down, g_final):
    batch, seq, _ = x.shape
    assert w_in.shape[0] == 1, "single-layer block: LAM_INIT and the call chain are for depth 1"
    layer = 0
    x2d =x.reshape(batch * seq, D_MODEL)
    cos_t, sin_t = _rope_tables(seq)

    later = [w_up_diff[layer], w_up_ret[layer], w_o[layer], w_ffn_gate[layer], w_ffn_up[layer],
             w_ffn_down[layer]]
    (dq, dk, dv, rq, rk, rv, rg, ga, gb, wa, wr, wo, wg, wu, wd) = _proj(
        x2d, g_mix[layer][None, :], w_in[layer], cos_t, sin_t, later, batch, seq)

    lam_params = jnp.stack([diff_lq1[layer], diff_lk1[layer], diff_lq2[layer], diff_lk2[layer]])
    a = _attn(lam_params.astype(F32), diff_subln_g[layer][None, :].astype(F32), dq, dk, dv)

    r = _ret(_decay_table(ret_decay_fwd[layer], ret_decay_bwd[layer]), rq, rk, rv, rg)

    y = _post(x2d, a, r, ga, gb, wa, wr, wo, g_ffn[layer][None, :], g_final[None, :], wg, wu, wd)
    return y.reshape(batch, seq, D_MODEL)
```

```python
import math

import jax
import jax.numpy as jnp
from jax import lax
from jax.experimental import pallas as pl
from jax.experimental.pallas import tpu as pltpu

F32 = jnp.float32
BF16 = jnp.bfloat16

D_MODEL = 1024
HEADS = 4
QK_DIM = 64
V_DIM = 128
DIFF_W = HEADS * 2 * QK_DIM
RET_QK_W = HEADS * QK_DIM
RET_V_W = HEADS * V_DIM
IN_COLS = 3 * DIFF_W + 2 * RET_QK_W + 2 * RET_V_W + 2 * D_MODEL
D_FF = 2816
ROPE_THETA = 10000.0
NORM_EPS = 1e-5
LAM_INIT = 0.8 - 0.6 * math.exp(-0.3 * 0)
LOG2E = 1.4426950408889634

LANES = 128
VMEM_LIMIT = 56 * 1024 * 1024
PROJ_TM = 512
PROJ_CAST_COLS = 512
N_LATER_WEIGHTS = 6
BF16_SUBLANES = 16
ATTN_TQ = 256
ATTN_HEADS_PER_STEP = 2
ATTN_ONES_ROWS = 16
ATTN_KB = 256
ATTN_PB = 256
RET_CHUNK = 256
POST_TM = 512


def _rms(x, eps=NORM_EPS):
    return x * lax.rsqrt(jnp.mean(x * x, axis=-1, keepdims=True) + eps)


def _const_spec(shape):
    nd = len(shape)
    return pl.BlockSpec(shape, lambda *_: (0,) * nd, pipeline_mode=pl.Buffered(1))


def _proj_kernel(x_ref, g_ref, wf_ref, cos_ref, sin_ref, *refs):
    later_f32 = refs[:N_LATER_WEIGHTS]
    dq_ref, dk_ref, dv_ref, rq_ref, rk_ref, rv_ref, rg_ref, ga_ref, gb_ref = refs[N_LATER_WEIGHTS:-N_LATER_WEIGHTS - 1]
    later_bf16 = refs[-N_LATER_WEIGHTS - 1:-1]
    wb_ref = refs[-1]
    tm = x_ref.shape[0]

    @pl.when(pl.program_id(0) == 0)
    def _():
        for c in range(0, IN_COLS, PROJ_CAST_COLS):
            wb_ref[:, c:c + PROJ_CAST_COLS] = wf_ref[:, c:c + PROJ_CAST_COLS].astype(BF16)

    for src_ref, dst_ref in zip(later_f32, later_bf16):
        dst_ref[...] = src_ref[...].astype(BF16)

    x = x_ref[...]
    hb = (x * g_ref[...]).astype(BF16)
    inv = lax.rsqrt(jnp.mean(x * x, axis=-1, keepdims=True) + NORM_EPS)
    cos = cos_ref[...]
    sin = sin_ref[...]
    lane = lax.broadcasted_iota(jnp.int32, (tm, LANES), 1)
    first_half = (lane % QK_DIM) < (QK_DIM // 2)

    def proj(c0, n):
        return jnp.dot(hb, wb_ref[:, c0:c0 + n], preferred_element_type=F32) * inv

    def rope(y, scale):
        partner = jnp.where(first_half, pltpu.roll(y, LANES - 32, 1), pltpu.roll(y, 32, 1))
        return ((y * cos + partner * sin) * scale).astype(BF16)

    q_scale = QK_DIM ** -0.5 * LOG2E
    y = proj(0, DIFF_W)
    for h in range(HEADS):
        dq_ref[h] = rope(y[:, h * LANES:(h + 1) * LANES], q_scale)
    y = proj(DIFF_W, DIFF_W)
    for h in range(HEADS):
        dk_ref[h] = rope(y[:, h * LANES:(h + 1) * LANES], 1.0)
    y = proj(2 * DIFF_W, DIFF_W)
    for h in range(HEADS):
        dv_ref[h] = y[:, h * LANES:(h + 1) * LANES].astype(BF16)
    c0 = 3 * DIFF_W
    y = proj(c0, 2 * RET_QK_W)
    for p in range(HEADS // 2):
        rq_ref[p] = rope(y[:, p * LANES:(p + 1) * LANES], 1.0)
        rk_ref[p] = rope(y[:, RET_QK_W + p * LANES:RET_QK_W + (p + 1) * LANES], QK_DIM ** -0.5)
    c0 += 2 * RET_QK_W
    y = proj(c0, RET_V_W)
    for p in range(HEADS // 2):
        rv_ref[p] = y[:, p * 2 * V_DIM:(p + 1) * 2 * V_DIM].astype(BF16)
    c0 += RET_V_W
    y = proj(c0, RET_V_W)
    for p in range(HEADS // 2):
        rg_ref[p] = y[:, p * 2 * V_DIM:(p + 1) * 2 * V_DIM].astype(BF16)
    c0 += RET_V_W
    for half in range(2):
        ga_ref[:, half * 512:(half + 1) * 512] = proj(c0 + half * 512, 512).astype(BF16)
    c0 += D_MODEL
    for half in range(2):
        gb_ref[:, half * 512:(half + 1) * 512] = proj(c0 + half * 512, 512).astype(BF16)


def _proj(x2d, g_mix, w_in, cos_t, sin_t, later_weights, batch, seq):
    tm = PROJ_TM
    nb = seq // tm
    tokens = batch * seq
    steps = tokens // tm
    assert len(later_weights) == N_LATER_WEIGHTS
    slab_specs = []
    for w in later_weights:
        n_slabs = max(n for n in range(1, steps + 1)
                      if w.shape[0] % n == 0 and (w.shape[0] // n) % BF16_SUBLANES == 0)
        slab_specs.append(pl.BlockSpec((w.shape[0] // n_slabs, w.shape[1]),
                                       lambda i, last=n_slabs - 1: (jnp.minimum(i, last), 0)))

    def head_major(n_groups, width):
        shape = jax.ShapeDtypeStruct((batch, n_groups, seq, width), BF16)
        spec = pl.BlockSpec((None, n_groups, tm, width), lambda i: (i // nb, 0, i % nb, 0))
        return shape, spec

    outs = [head_major(HEADS, LANES)] * 3 + [head_major(HEADS // 2, LANES)] * 2 \
        + [head_major(HEADS // 2, 2 * V_DIM)] * 2
    out_shape = [o[0] for o in outs] + [jax.ShapeDtypeStruct((tokens, D_MODEL), BF16)] * 2 \
        + [jax.ShapeDtypeStruct(w.shape, BF16) for w in later_weights]
    out_specs = [o[1] for o in outs] + [pl.BlockSpec((tm, D_MODEL), lambda i: (i, 0))] * 2 + slab_specs
    return pl.pallas_call(
        _proj_kernel,
        grid=(steps,),
        in_specs=[
            pl.BlockSpec((tm, D_MODEL), lambda i: (i, 0)),
            _const_spec((1, D_MODEL)),
            _const_spec((D_MODEL, IN_COLS)),
            pl.BlockSpec((tm, LANES), lambda i: (i % nb, 0)),
            pl.BlockSpec((tm, LANES), lambda i: (i % nb, 0)),
        ] + slab_specs,
        out_specs=out_specs,
        out_shape=out_shape,
        scratch_shapes=[pltpu.VMEM((D_MODEL, IN_COLS), BF16)],
        compiler_params=pltpu.CompilerParams(
            dimension_semantics=("arbitrary",), vmem_limit_bytes=VMEM_LIMIT),
        name="in_proj",
    )(x2d, g_mix, w_in, cos_t, sin_t, *later_weights)


def _attn_kernel(lam_ref, g_ref, q_ref, k_ref, v_ref, o_ref, vext_ref, *scratch):
    n_heads, seq = q_ref.shape[0], q_ref.shape[1]
    tq = ATTN_TQ
    nq = seq // tq
    nt_dims = (((1,), (1,)), ((), ()))

    for hh in range(n_heads):
        vext_ref[hh, :V_DIM, :] = v_ref[hh].astype(F32).T.astype(BF16)
        vext_ref[hh, V_DIM:, :] = jnp.ones((ATTN_ONES_ROWS, seq), BF16)

    lp = lam_ref[...]
    lam = (jnp.exp(jnp.sum(lp[0:1] * lp[1:2], axis=-1, keepdims=True))
           - jnp.exp(jnp.sum(lp[2:3] * lp[3:4], axis=-1, keepdims=True)) + LAM_INIT)
    lane = lax.broadcasted_iota(jnp.int32, (tq, LANES), 1)

    bufs = tuple(tuple(tuple(scratch[(hh * 2 + par) * 3:(hh * 2 + par) * 3 + 3]) for par in range(2))
                 for hh in range(n_heads))
    heads = range(n_heads)

    def step(qk=None, pv=None, fin=None):
        if fin is not None:
            j, par = fin
            for hh in heads:
                acc = bufs[hh][par][2][...]
                o = acc[:V_DIM] / acc[V_DIM:V_DIM + 1]
                o = (o[:, :tq] - lam * o[:, tq:]).T
                o = _rms(o) * g_ref[...] * (1.0 - LAM_INIT)
                o_ref[pl.ds(pl.multiple_of(j * tq, tq), tq), hh * V_DIM:(hh + 1) * V_DIM] = o.astype(BF16)
        if qk is not None:
            i, qpar = qk
            q2, m = [], [None] * n_heads
            for hh in heads:
                q = q_ref[hh, pl.ds(pl.multiple_of(i * tq, tq), tq), :]
                zero = jnp.zeros_like(q)
                q2.append(jnp.concatenate([jnp.where(lane < QK_DIM, q, zero),
                                           jnp.where(lane >= QK_DIM, q, zero)], axis=0))
        if pv is not None:
            _, ppar = pv
            m_prev = [bufs[hh][ppar][1][...] for hh in heads]
            acc = [None] * n_heads
        for kb in range(seq // ATTN_KB):
            rows = slice(kb * ATTN_KB, (kb + 1) * ATTN_KB)
            for hh in heads:
                if qk is not None:
                    s = lax.dot_general(k_ref[hh, rows, :], q2[hh], nt_dims, preferred_element_type=F32)
                    bufs[hh][qpar][0][rows, :] = s
                    blk = jnp.max(s, axis=0, keepdims=True)
                    m[hh] = blk if m[hh] is None else jnp.maximum(m[hh], blk)
                if pv is not None:
                    for sub in range(ATTN_KB // ATTN_PB):
                        r0 = kb * ATTN_KB + sub * ATTN_PB
                        rows2 = slice(r0, r0 + ATTN_PB)
                        p = jnp.exp2(bufs[hh][ppar][0][rows2, :] - m_prev[hh]).astype(BF16)
                        part = jnp.dot(vext_ref[hh, :, rows2], p, preferred_element_type=F32)
                        acc[hh] = part if acc[hh] is None else acc[hh] + part
        for hh in heads:
            if qk is not None:
                bufs[hh][qpar][1][...] = m[hh]
            if pv is not None:
                bufs[hh][ppar][2][...] = acc[hh]

    def unit(t, parity):
        return t, parity

    step(qk=unit(0, 0))
    step(qk=unit(1, 1), pv=unit(0, 0))

    def body(jj, carry):
        t = 2 * jj
        step(qk=unit(t, 0), pv=unit(t - 1, 1), fin=unit(t - 2, 0))
        step(qk=unit(t + 1, 1), pv=unit(t, 0), fin=unit(t - 1, 1))
        return carry

    lax.fori_loop(1, nq // 2, body, 0)

    step(pv=unit(nq - 1, 1), fin=unit(nq - 2, 0))
    step(fin=unit(nq - 1, 1))


def _attn(lam_params, g_sub, dq, dk, dv):
    batch, heads, seq, _ = dq.shape
    tq = ATTN_TQ
    hp = ATTN_HEADS_PER_STEP
    assert (seq // tq) % 2 == 0 and heads % hp == 0
    seq_spec = pl.BlockSpec((None, hp, seq, LANES), lambda b, p: (b, p, 0, 0))
    unit_bufs = [pltpu.VMEM((seq, 2 * tq), F32), pltpu.VMEM((1, 2 * tq), F32),
                 pltpu.VMEM((V_DIM + ATTN_ONES_ROWS, 2 * tq), F32)]
    return pl.pallas_call(
        _attn_kernel,
        grid=(batch, heads // hp),
        in_specs=[_const_spec((4, QK_DIM)), _const_spec((1, V_DIM)), seq_spec, seq_spec, seq_spec],
        out_specs=pl.BlockSpec((seq, hp * V_DIM), lambda b, p: (b, p)),
        out_shape=jax.ShapeDtypeStruct((batch * seq, HEADS * V_DIM), BF16),
        scratch_shapes=[pltpu.VMEM((hp, V_DIM + ATTN_ONES_ROWS, seq), BF16)] + unit_bufs * (2 * hp),
        compiler_params=pltpu.CompilerParams(
            dimension_semantics=("parallel", "parallel"), vmem_limit_bytes=VMEM_LIMIT),
        name="diff_attn",
    )(lam_params, g_sub, dq, dk, dv)


def _log_sigmoid(x):
    return jnp.minimum(x, 0.0) - jnp.log1p(jnp.exp(-jnp.abs(x)))


def _ret_kernel(dec_ref, q_ref, k_ref, v_ref, rg_ref, o_ref, dmat_ref, u_ref, sf_ref, sb_ref):
    seq = q_ref.shape[0]
    c_len = RET_CHUNK
    n_chunks = seq // c_len
    pair_w = 2 * V_DIM
    nt_dims = (((1,), (1,)), ((), ()))
    tn_dims = (((0,), (0,)), ((), ()))

    lg = _log_sigmoid(dec_ref[...])
    row_i = lax.broadcasted_iota(jnp.int32, (c_len, c_len), 0)
    col_i = lax.broadcasted_iota(jnp.int32, (c_len, c_len), 1)
    diff = (row_i - col_i).astype(F32)
    for hh in range(2):
        dmat_ref[hh * c_len:(hh + 1) * c_len, :] = jnp.exp(
            jnp.where(diff >= 0, diff * lg[4 + 2 * hh:5 + 2 * hh], -diff * lg[5 + 2 * hh:6 + 2 * hh]))
    rows_k = lax.broadcasted_iota(jnp.int32, (c_len, LANES), 0).astype(F32)
    rows_v = lax.broadcasted_iota(jnp.int32, (c_len, pair_w), 0).astype(F32)
    zeta_f = jnp.exp((c_len - 1 - rows_k) * lg[0:1, :LANES])
    zeta_b = jnp.exp(rows_k * lg[1:2, :LANES])
    xi_f = jnp.exp((rows_v + 1) * lg[2:3])
    xi_b = jnp.exp((c_len - rows_v) * lg[3:4])
    srow = lax.broadcasted_iota(jnp.int32, (LANES, pair_w), 0)
    scol = lax.broadcasted_iota(jnp.int32, (LANES, pair_w), 1)
    head_a_row = srow < QK_DIM
    same_head = head_a_row == (scol < V_DIM)
    g_f = jnp.exp(c_len * jnp.where(head_a_row, lg[4:5], lg[6:7]))
    g_b = jnp.exp(c_len * jnp.where(head_a_row, lg[5:6], lg[7:8]))
    lane = lax.broadcasted_iota(jnp.int32, (c_len, LANES), 1)

    def chunk(c):
        return pl.ds(pl.multiple_of(c * c_len, c_len), c_len)

    def phase_a(c, carry):
        kc = k_ref[chunk(c), :].astype(F32)
        kz = jnp.concatenate([kc * zeta_f, kc * zeta_b], axis=1).astype(BF16)
        u_ref[c] = lax.dot_general(kz, v_ref[chunk(c), :], tn_dims, preferred_element_type=F32)
        return carry

    lax.fori_loop(0, n_chunks, phase_a, 0, unroll=16)

    def phase_b(t, carry):
        s_f, s_b = carry
        sf_ref[t] = jnp.where(same_head, s_f, 0.0).astype(BF16)
        s_f = g_f * s_f + u_ref[t, :LANES, :]
        cb = n_chunks - 1 - t
        sb_ref[cb] = jnp.where(same_head, s_b, 0.0).astype(BF16)
        s_b = g_b * s_b + u_ref[cb, LANES:, :]
        return s_f, s_b

    zero_state = jnp.zeros((LANES, pair_w), F32)
    lax.fori_loop(0, n_chunks, phase_b, (zero_state, zero_state))

    def phase_c(c, carry):
        q = q_ref[chunk(c), :]
        zero = jnp.zeros_like(q)
        q2 = jnp.concatenate([jnp.where(lane < QK_DIM, q, zero),
                              jnp.where(lane >= QK_DIM, q, zero)], axis=0)
        vc = v_ref[chunk(c), :]
        sc = lax.dot_general(q2, k_ref[chunk(c), :], nt_dims, preferred_element_type=F32) * dmat_ref[...]
        inner = jnp.dot(sc.astype(BF16), vc, preferred_element_type=F32)
        tot = (jnp.concatenate([inner[:c_len, :V_DIM], inner[c_len:, V_DIM:]], axis=1)
               + jnp.dot(q, sf_ref[c], preferred_element_type=F32) * xi_f
               + jnp.dot(q, sb_ref[c], preferred_element_type=F32) * xi_b)
        r = jnp.concatenate([_rms(tot[:, :V_DIM]), _rms(tot[:, V_DIM:])], axis=1)
        gate = rg_ref[chunk(c), :].astype(F32)
        o_ref[chunk(c), :] = (r * gate * jax.nn.sigmoid(gate)).astype(BF16)
        return carry

    lax.fori_loop(0, n_chunks, phase_c, 0, unroll=16)


def _ret(dec, rq, rk, rv, rg):
    batch, pairs, seq, _ = rq.shape
    pair_w = 2 * V_DIM
    n_chunks = seq // RET_CHUNK
    assert RET_CHUNK == pair_w
    return pl.pallas_call(
        _ret_kernel,
        grid=(batch, pairs),
        in_specs=[
            pl.BlockSpec((None, 8, pair_w), lambda b, p: (p, 0, 0)),
            pl.BlockSpec((None, None, seq, LANES), lambda b, p: (b, p, 0, 0)),
            pl.BlockSpec((None, None, seq, LANES), lambda b, p: (b, p, 0, 0)),
            pl.BlockSpec((None, None, seq, pair_w), lambda b, p: (b, p, 0, 0)),
            pl.BlockSpec((None, None, seq, pair_w), lambda b, p: (b, p, 0, 0)),
        ],
        out_specs=pl.BlockSpec((seq, pair_w), lambda b, p: (b, p)),
        out_shape=jax.ShapeDtypeStruct((batch * seq, RET_V_W), BF16),
        scratch_shapes=[
            pltpu.VMEM((2 * RET_CHUNK, RET_CHUNK), F32),
            pltpu.VMEM((n_chunks, 2 * LANES, pair_w), F32),
            pltpu.VMEM((n_chunks, LANES, pair_w), BF16),
            pltpu.VMEM((n_chunks, LANES, pair_w), BF16),
        ],
        compiler_params=pltpu.CompilerParams(
            dimension_semantics=("parallel", "parallel"), vmem_limit_bytes=VMEM_LIMIT),
        name="retention",
    )(dec, rq, rk, rv, rg)


def _post_kernel(x_ref, a_ref, r_ref, ga_ref, gb_ref, wa_ref, wr_ref, wo_ref,
                 g2_ref, gf_ref, wg_ref, wu_ref, wd_ref, o_ref):
    ya = jnp.dot(a_ref[...], wa_ref[...], preferred_element_type=F32)
    yb = jnp.dot(r_ref[...], wr_ref[...], preferred_element_type=F32)
    m = (jax.nn.sigmoid(ga_ref[...].astype(F32)) * ya
         + jax.nn.sigmoid(gb_ref[...].astype(F32)) * yb)
    x = x_ref[...] + jnp.dot(m.astype(BF16), wo_ref[...], preferred_element_type=F32)
    inv = lax.rsqrt(jnp.mean(x * x, axis=-1, keepdims=True) + NORM_EPS)
    h = (x * g2_ref[...]).astype(BF16)
    gate = jnp.dot(h, wg_ref[...], preferred_element_type=F32) * inv
    up = jnp.dot(h, wu_ref[...], preferred_element_type=F32) * inv
    act = (gate * jax.nn.sigmoid(gate) * up).astype(BF16)
    x = x + jnp.dot(act, wd_ref[...], preferred_element_type=F32)
    o_ref[...] = _rms(x) * gf_ref[...]


def _post(x2d, a, r, ga, gb, wa, wr, wo, g_ffn, g_final, wg, wu, wd):
    tokens = x2d.shape[0]
    tm = POST_TM
    row = lambda w: pl.BlockSpec((tm, w), lambda i: (i, 0))
    consts = [wa, wr, wo, g_ffn, g_final, wg, wu, wd]
    return pl.pallas_call(
        _post_kernel,
        grid=(tokens // tm,),
        in_specs=[row(D_MODEL), row(DIFF_W), row(RET_V_W), row(D_MODEL), row(D_MODEL)]
        + [_const_spec(c.shape) for c in consts],
        out_specs=row(D_MODEL),
        out_shape=jax.ShapeDtypeStruct((tokens, D_MODEL), F32),
        compiler_params=pltpu.CompilerParams(
            dimension_semantics=("parallel",), vmem_limit_bytes=VMEM_LIMIT),
        name="post",
    )(x2d, a, r, ga, gb, *consts)


def _rope_tables(seq):
    half = QK_DIM // 2
    inv_freq = ROPE_THETA ** (-jnp.arange(0, QK_DIM, 2, dtype=F32) / QK_DIM)
    ang = jnp.arange(seq, dtype=F32)[:, None] * inv_freq[None, :]
    cos, sin = jnp.cos(ang), jnp.sin(ang)
    cos_t = jnp.tile(cos, (1, LANES // half))
    sin_t = jnp.tile(jnp.concatenate([-sin, sin], axis=-1), (1, LANES // QK_DIM))
    return cos_t, sin_t


def _decay_table(fwd, bwd):
    pairs = HEADS // 2
    f = fwd.astype(F32).reshape(pairs, 2)
    b = bwd.astype(F32).reshape(pairs, 2)
    k_lanes = lambda t: jnp.tile(jnp.repeat(t, QK_DIM, axis=1), (1, 2))
    v_lanes = lambda t: jnp.repeat(t, V_DIM, axis=1)
    all_lanes = lambda col: jnp.broadcast_to(col[:, None], (pairs, 2 * V_DIM))
    rows = [k_lanes(f), k_lanes(b), v_lanes(f), v_lanes(b),
            all_lanes(f[:, 0]), all_lanes(b[:, 0]), all_lanes(f[:, 1]), all_lanes(b[:, 1])]
    return jnp.stack(rows, axis=1)


def kernel(x, g_mix, w_in, diff_lq1, diff_lk1, diff_lq2, diff_lk2, diff_subln_g,
           ret_decay_fwd, ret_decay_bwd, w_up_diff, w_up_ret, w_o,
           g_ffn, w_ffn_gate, w_ffn_up, w_ffn_down, g_final):
    batch, seq, _ = x.shape
    assert w_in.shape[0] == 1, "single-layer block: LAM_INIT and the call chain are for depth 1"
    layer = 0
    x2d =x.reshape(batch * seq, D_MODEL)
    cos_t, sin_t = _rope_tables(seq)

    later = [w_up_diff[layer], w_up_ret[layer], w_o[layer], w_ffn_gate[layer], w_ffn_up[layer],
             w_ffn_down[layer]]
    (dq, dk, dv, rq, rk, rv, rg, ga, gb, wa, wr, wo, wg, wu, wd) = _proj(
        x2d, g_mix[layer][None, :], w_in[layer], cos_t, sin_t, later, batch, seq)

    lam_params = jnp.stack([diff_lq1[layer], diff_lk1[layer], diff_lq2[layer], diff_lk2[layer]])
    a = _attn(lam_params.astype(F32), diff_subln_g[layer][None, :].astype(F32), dq, dk, dv)

    r = _ret(_decay_table(ret_decay_fwd[layer], ret_decay_bwd[layer]), rq, rk, rv, rg)

    y = _post(x2d, a, r, ga, gb, wa, wr, wo, g_ffn[layer][None, :], g_final[None, :], wg, wu, wd)
    return y.reshape(batch, seq, D_MODEL)
```
